```python
import math
import jax, jax.numpy as jnp
from jax import lax
import numpy as np

D_MODEL = 1024
BATCH = 8
SEQ = 8192
DEPTH = 1

GRID_W = 64
EPS = 1e-6
ATTN_HEADS = 8
ATTN_KV_HEADS = 2
GQA_GROUP = ATTN_HEADS // ATTN_KV_HEADS
HEAD_DIM = 64
ROPE_THETA = 10000.0
ROPE_AXIS_DIM = HEAD_DIM // 2
Q_BLOCK = 128
HGRN_HEADS = 4
HGRN_DK = 128
HGRN_DV = 128
HGRN_CHUNK = 32
HGRN_SCALE = HGRN_DK ** -0.5
N_GROUPS = 4
EXPERTS_PER_GROUP = 8
N_EXPERTS = N_GROUPS * EXPERTS_PER_GROUP
TOP_K = 2
D_EXPERT = 512

ATTN_Q_W = ATTN_HEADS * HEAD_DIM
ATTN_KV_W = ATTN_KV_HEADS * HEAD_DIM
HGRN_K_W = HGRN_HEADS * HGRN_DK
HGRN_V_W = HGRN_HEADS * HGRN_DV
IN_SPLITS = (ATTN_Q_W, ATTN_KV_W, ATTN_KV_W, HGRN_K_W, HGRN_K_W, HGRN_K_W, HGRN_V_W, HGRN_V_W, D_MODEL, D_MODEL)
D_IN = sum(IN_SPLITS)
SPLIT_POINTS = tuple(int(s) for s in np.cumsum(IN_SPLITS)[:-1])

kernel_name = "hybrid_gqa_hgrn2_hiermoe_encoder"


def rmsnorm(x, g):
    xf = x.astype(jnp.float32)
    y = xf * lax.rsqrt(jnp.mean(xf * xf, axis=-1, keepdims=True) + EPS)
    return (y * g.astype(jnp.float32)).astype(x.dtype)


def axial_rope_tables(seq):
    rows = seq // GRID_W
    row_ids = jnp.repeat(jnp.arange(rows), GRID_W).astype(jnp.float32)
    col_ids = jnp.tile(jnp.arange(GRID_W), rows).astype(jnp.float32)
    inv_freq = ROPE_THETA ** (-jnp.arange(0, ROPE_AXIS_DIM, 2, dtype=jnp.float32) / ROPE_AXIS_DIM)
    ang = jnp.concatenate([row_ids[:, None] * inv_freq, col_ids[:, None] * inv_freq], axis=-1)
    return jnp.cos(ang), jnp.sin(ang)


def apply_rope(x, cos, sin):
    xr = x.astype(jnp.float32).reshape(x.shape[:-1] + (HEAD_DIM // 2, 2))
    a, b = xr[..., 0], xr[..., 1]
    out = jnp.stack([a * cos - b * sin, a * sin + b * cos], axis=-1)
    return out.reshape(x.shape).astype(x.dtype)


def bidir_gqa(q, k, v, q_norm, k_norm):
    B, S, _ = q.shape
    q = rmsnorm(q.reshape(B, S, ATTN_KV_HEADS, GQA_GROUP, HEAD_DIM), q_norm)
    k = rmsnorm(k.reshape(B, S, ATTN_KV_HEADS, HEAD_DIM), k_norm)
    v = v.reshape(B, S, ATTN_KV_HEADS, HEAD_DIM)
    cos, sin = axial_rope_tables(S)
    q = apply_rope(q, cos[:, None, None, :], sin[:, None, None, :])
    k = apply_rope(k, cos[:, None, :], sin[:, None, :])
    nb = S // Q_BLOCK
    qb = q.reshape(B, nb, Q_BLOCK, ATTN_KV_HEADS, GQA_GROUP, HEAD_DIM).transpose(1, 0, 3, 4, 2, 5)
    kt = k.transpose(0, 2, 1, 3)
    vt = v.transpose(0, 2, 1, 3)
    scale = HEAD_DIM ** -0.5

    def one_block(qblk):
        s = jnp.einsum('bhgqd,bhkd->bhgqk', qblk, kt).astype(jnp.float32) * scale
        p = jax.nn.softmax(s, axis=-1).astype(vt.dtype)
        return jnp.einsum('bhgqk,bhkd->bhgqd', p, vt)

    o = lax.map(one_block, qb)
    return o.transpose(1, 0, 4, 2, 3, 5).reshape(B, S, ATTN_Q_W)


def gla_chunkwise(q, k, v, log_f):
    B, H, S, dk = q.shape
    dv = v.shape[-1]
    nc = S // HGRN_CHUNK
    C = HGRN_CHUNK
    q = q.reshape(B, H, nc, C, dk)
    k = k.reshape(B, H, nc, C, dk)
    v = v.reshape(B, H, nc, C, dv)
    b = jnp.cumsum(log_f.reshape(B, H, nc, C, dk), axis=3)
    b_last = b[:, :, :, C - 1:, :]
    b_ref = b[:, :, :, C // 2:C // 2 + 1, :]
    a = jnp.einsum('bhnck,bhnsk->bhncs', q * jnp.exp(b - b_ref), k * jnp.exp(b_ref - b))
    mask = jnp.tril(jnp.ones((C, C), dtype=bool))
    o_intra = jnp.einsum('bhncs,bhnsv->bhncv', jnp.where(mask, a, 0.0), v)
    q_inter = q * jnp.exp(b)
    k_state = k * jnp.exp(b_last - b)
    decay = jnp.exp(b_last[:, :, :, 0, :])
    xs = (jnp.moveaxis(q_inter, 2, 0), jnp.moveaxis(k_state, 2, 0), jnp.moveaxis(v, 2, 0), jnp.moveaxis(decay, 2, 0))

    def step(state, inp):
        qi, ks, vv, d = inp
        o = jnp.einsum('bhck,bhkv->bhcv', qi, state)
        state = d[..., None] * state + jnp.einsum('bhck,bhcv->bhkv', ks, vv)
        return state, o

    s0 = jnp.zeros((B, H, dk, dv), jnp.float32)
    _, o_inter = lax.scan(step, s0, xs)
    o = o_intra + jnp.moveaxis(o_inter, 0, 2)
    return o.reshape(B, H, S, dv)


def hgrn2_bidir(hq, f_fwd, f_bwd, i_in, out_gate, lb_f, lb_b, hgrn_norm):
    B, S, _ = hq.shape

    def heads(t, d):
        return t.reshape(B, S, HGRN_HEADS, d).transpose(0, 2, 1, 3).astype(jnp.float32)

    q = heads(jax.nn.silu(hq.astype(jnp.float32)) * HGRN_SCALE, HGRN_DK)
    v = heads(i_in, HGRN_DV)

    def direction(f_logits, lb, flip):
        z = f_logits.astype(jnp.float32)
        f = lb + (1.0 - lb) * jax.nn.sigmoid(z)
        k = heads((1.0 - lb) * jax.nn.sigmoid(-z), HGRN_DK)
        log_f = heads(jnp.log(f), HGRN_DK)
        if flip:
            o = gla_chunkwise(jnp.flip(q, 2), jnp.flip(k, 2), jnp.flip(v, 2), jnp.flip(log_f, 2))
            return jnp.flip(o, 2)
        return gla_chunkwise(q, k, v, log_f)

    o = direction(f_fwd, lb_f, False) + direction(f_bwd, lb_b, True)
    o = rmsnorm(o.transpose(0, 2, 1, 3), hgrn_norm)
    o = o * jax.nn.silu(out_gate.reshape(B, S, HGRN_HEADS, HGRN_DV).astype(jnp.float32))
    return o.reshape(B, S, HGRN_V_W).astype(hq.dtype)


def hier_moe(h, w_rg, b_rg, w_re, b_re, w_gate, w_up, w_down):
    B, S, D = h.shape
    t = h.reshape(B * S, D)
    g_prob = jax.nn.softmax((t @ w_rg + b_rg).astype(jnp.float32), axis=-1)
    p_g, g_idx = lax.top_k(g_prob, 1)
    e_logits = (t @ w_re + b_re).astype(jnp.float32).reshape(-1, N_GROUPS, EXPERTS_PER_GROUP)
    e_sel = jnp.take_along_axis(e_logits, g_idx[:, :, None], axis=1)[:, 0]
    p_e, e_idx = lax.top_k(jax.nn.softmax(e_sel, axis=-1), TOP_K)
    p_e = p_e / jnp.sum(p_e, axis=-1, keepdims=True)
    expert_ids = g_idx * EXPERTS_PER_GROUP + e_idx
    combine = p_g * p_e
    dense_w = jnp.sum(jax.nn.one_hot(expert_ids, N_EXPERTS, dtype=jnp.float32) * combine[..., None], axis=1)
    dense_w = dense_w.astype(t.dtype)
    y = jnp.zeros((B * S, D), jnp.float32)
    for e in range(N_EXPERTS):
        hid = jax.nn.silu(t @ w_gate[e]) * (t @ w_up[e])
        y = y + ((hid * dense_w[:, e:e + 1]) @ w_down[e]).astype(jnp.float32)
    return y.reshape(B, S, D).astype(h.dtype)


def setup_inputs(seed: int = 0) -> dict:
    key = jax.random.key(seed)
    ks = jax.random.split(key, 20)
    f32 = jnp.float32
    L = DEPTH

    def nrm(k, shape, scale):
        return jax.random.normal(k, shape, f32) * scale

    return {
        "x": nrm(ks[0], (BATCH, SEQ, D_MODEL), 1.0),
        "g_mix": 1.0 + nrm(ks[1], (L, D_MODEL), 0.02),
        "w_in": nrm(ks[2], (L, D_MODEL, D_IN), D_MODEL ** -0.5),
        "q_norm": 1.0 + nrm(ks[3], (L, HEAD_DIM), 0.02),
        "k_norm": 1.0 + nrm(ks[4], (L, HEAD_DIM), 0.02),
        "hgrn_norm": 1.0 + nrm(ks[5], (L, HGRN_DV), 0.02),
        "lb_fwd": nrm(ks[6], (L + 1, HGRN_K_W), 0.5),
        "lb_bwd": nrm(ks[7], (L + 1, HGRN_K_W), 0.5),
        "w_attn_branch": nrm(ks[8], (L, ATTN_Q_W, D_MODEL), ATTN_Q_W ** -0.5),
        "w_hgrn_branch": nrm(ks[9], (L, HGRN_V_W, D_MODEL), HGRN_V_W ** -0.5),
        "w_out": nrm(ks[10], (L, D_MODEL, D_MODEL), D_MODEL ** -0.5),
        "g_ffn": 1.0 + nrm(ks[11], (L, D_MODEL), 0.02),
        "w_router_group": nrm(ks[12], (L, D_MODEL, N_GROUPS), D_MODEL ** -0.5),
        "b_router_group": nrm(ks[13], (L, N_GROUPS), 0.01),
        "w_router_expert": nrm(ks[14], (L, D_MODEL, N_EXPERTS), D_MODEL ** -0.5),
        "b_router_expert": nrm(ks[15], (L, N_EXPERTS), 0.01),
        "w_exp_gate": nrm(ks[16], (L, N_EXPERTS, D_MODEL, D_EXPERT), D_MODEL ** -0.5),
        "w_exp_up": nrm(ks[17], (L, N_EXPERTS, D_MODEL, D_EXPERT), D_MODEL ** -0.5),
        "w_exp_down": nrm(ks[18], (L, N_EXPERTS, D_EXPERT, D_MODEL), D_EXPERT ** -0.5),
    }


def reference(x, g_mix, w_in, q_norm, k_norm, hgrn_norm, lb_fwd, lb_bwd, w_attn_branch, w_hgrn_branch,
              w_out, g_ffn, w_router_group, b_router_group, w_router_expert, b_router_expert,
              w_exp_gate, w_exp_up, w_exp_down):
    lbs_f = jnp.cumsum(jax.nn.softmax(lb_fwd.astype(jnp.float32), axis=0), axis=0)
    lbs_b = jnp.cumsum(jax.nn.softmax(lb_bwd.astype(jnp.float32), axis=0), axis=0)
    for l in range(DEPTH):
        h = rmsnorm(x, g_mix[l])
        proj = h @ w_in[l]
        aq, ak, av, hq, hf_f, hf_b, hi, hg, gate_a, gate_b = jnp.split(proj, SPLIT_POINTS, axis=-1)
        attn_o = bidir_gqa(aq, ak, av, q_norm[l], k_norm[l])
        hgrn_o = hgrn2_bidir(hq, hf_f, hf_b, hi, hg, lbs_f[l], lbs_b[l], hgrn_norm[l])
        merged = (jax.nn.sigmoid(gate_a) * (attn_o @ w_attn_branch[l])
                  + jax.nn.sigmoid(gate_b) * (hgrn_o @ w_hgrn_branch[l]))
        x = x + merged @ w_out[l]
        h = rmsnorm(x, g_ffn[l])
        x = x + hier_moe(h, w_router_group[l], b_router_group[l], w_router_expert[l], b_router_expert[l],
                         w_exp_gate[l], w_exp_up[l], w_exp_down[l])
    return x
```

```python
import functools
import math

import jax
import jax.numpy as jnp
import numpy as np
from jax import lax
from jax.experimental import pallas as pl
from jax.experimental.pallas import tpu as pltpu

F32 = jnp.float32
BF16 = jnp.bfloat16

D_MODEL = 1024
GRID_W = 64
EPS = 1e-6
ATTN_HEADS = 8
ATTN_KV_HEADS = 2
GQA_GROUP = ATTN_HEADS // ATTN_KV_HEADS
HEAD_DIM = 64
HALF = HEAD_DIM // 2
ROPE_THETA = 10000.0
HGRN_HEADS = 4
HGRN_DK = 128
HGRN_CHUNK = 32
HGRN_SCALE = HGRN_DK ** -0.5
N_GROUPS = 4
EPG = 8
N_EXPERTS = N_GROUPS * EPG
D_EXPERT = 512
ATTN_Q_W = ATTN_HEADS * HEAD_DIM
ATTN_KV_W = ATTN_KV_HEADS * HEAD_DIM
HGRN_W = HGRN_HEADS * HGRN_DK
IN_SPLITS = (ATTN_Q_W, ATTN_KV_W, ATTN_KV_W, HGRN_W, HGRN_W, HGRN_W, HGRN_W, HGRN_W, D_MODEL, D_MODEL)
IN_OFF = tuple(int(v) for v in np.cumsum((0,) + IN_SPLITS))
D_IN = IN_OFF[-1]
LANES = 128
ROUTE_W = LANES
EXPERT_ROW0 = 8
VMEM_LIMIT = 56 * 1024 * 1024
NEG_BIG = -1e30


def _sigmoid(x):
    return 1.0 / (1.0 + jnp.exp(-x))


def _silu(x):
    return x * _sigmoid(x)


def _dot(a, b):
    return jnp.dot(a, b, preferred_element_type=F32)


def _params(sem):
    return pltpu.CompilerParams(dimension_semantics=sem, vmem_limit_bytes=VMEM_LIMIT)


def _const_spec(shape):
    nd = len(shape)
    return pl.BlockSpec(shape, lambda *_: (0,) * nd)


def _inproj_kernel(x_ref, g_ref, w_ref, gq_ref, gk_ref, cq_ref, sq_ref, ckt_ref, skt_ref,
                   lbf_ref, lbb_ref, e64_ref,
                   q_ref, kt_ref, v_ref, hq_ref, hv_ref, kf_ref, lff_ref, kb_ref, lfb_ref,
                   og_ref, sa_ref, sb_ref):
    tm = x_ref.shape[0]
    x = x_ref[...]
    ms = jnp.mean(x * x, axis=-1, keepdims=True)
    h = (x * lax.rsqrt(ms + EPS) * g_ref[...]).astype(BF16)

    def proj(i):
        return _dot(h, w_ref[:, IN_OFF[i]:IN_OFF[i + 1]])

    q = proj(0)
    ss = _dot((q * q).astype(BF16), e64_ref[...])
    qn = q * lax.rsqrt(ss * (1.0 / HEAD_DIM) + EPS) * gq_ref[...]
    lane = lax.broadcasted_iota(jnp.int32, qn.shape, 1)
    partner = jnp.where((lane & HALF) == 0,
                        pltpu.roll(qn, ATTN_Q_W - HALF, 1), pltpu.roll(qn, HALF, 1))
    reps = ATTN_Q_W // cq_ref.shape[1]
    qr = (qn * jnp.tile(cq_ref[...], (1, reps)) + partner * jnp.tile(sq_ref[...], (1, reps)))
    qr = qr * (HEAD_DIM ** -0.5)
    for hh in range(ATTN_HEADS):
        q_ref[0, hh] = qr[:, HEAD_DIM * hh:HEAD_DIM * (hh + 1)].astype(BF16)

    k = proj(1)
    kt = k.T.reshape(ATTN_KV_HEADS, HEAD_DIM, tm)
    ssk = jnp.sum(kt * kt, axis=1, keepdims=True)
    kn = kt * lax.rsqrt(ssk * (1.0 / HEAD_DIM) + EPS) * gk_ref[...][None]
    a, b = kn[:, :HALF, :], kn[:, HALF:, :]
    c, s = ckt_ref[...][None], skt_ref[...][None]
    kt_ref[0, :, 0] = jnp.concatenate([a * c - b * s, a * s + b * c], axis=1).astype(BF16)
    v = proj(2)
    for gg in range(ATTN_KV_HEADS):
        v_ref[0, gg] = v[:, HEAD_DIM * gg:HEAD_DIM * (gg + 1)].astype(BF16)

    hq_ref[...] = (_silu(proj(3)) * HGRN_SCALE).astype(BF16)

    def direction(z, lbraw_ref, k_out, lf_out):
        raw = lbraw_ref[...]
        e = jnp.exp(raw - jnp.max(raw, axis=0, keepdims=True))
        lb = e[0:1] / jnp.sum(e, axis=0, keepdims=True)
        k_out[...] = ((1.0 - lb) * _sigmoid(-z)).astype(BF16)
        lf_out[...] = jnp.log(lb + (1.0 - lb) * _sigmoid(z))

    direction(proj(4), lbf_ref, kf_ref, lff_ref)
    direction(proj(5), lbb_ref, kb_ref, lfb_ref)
    hv_ref[...] = proj(6).astype(BF16)
    og_ref[...] = _silu(proj(7)).astype(BF16)
    sa_ref[...] = _sigmoid(proj(8)).astype(BF16)
    sb_ref[...] = _sigmoid(proj(9)).astype(BF16)


def _in_proj(x2d, g_mix, w_all, gq, gk, cq, sq, ckt, skt, lb_fwd, lb_bwd, e64, *, B, S, tm):
    T = B * S
    nsb = S // tm
    tok = lambda w: pl.BlockSpec((tm, w), lambda i: (i, 0))
    tab = lambda w: pl.BlockSpec((tm, w), lambda i: (i % nsb, 0))
    in_specs = [
        tok(D_MODEL), _const_spec((1, D_MODEL)), _const_spec((D_MODEL, D_IN)),
        _const_spec((1, ATTN_Q_W)), _const_spec((HEAD_DIM, 1)),
        tab(LANES), tab(LANES),
        pl.BlockSpec((HALF, tm), lambda i: (0, i % nsb)), pl.BlockSpec((HALF, tm), lambda i: (0, i % nsb)),
        _const_spec(lb_fwd.shape), _const_spec(lb_bwd.shape), _const_spec((ATTN_Q_W, ATTN_Q_W)),
    ]
    out_shape = [
        jax.ShapeDtypeStruct((B, ATTN_HEADS, S, HEAD_DIM), BF16),
        jax.ShapeDtypeStruct((B, ATTN_KV_HEADS, nsb, HEAD_DIM, tm), BF16),
        jax.ShapeDtypeStruct((B, ATTN_KV_HEADS, S, HEAD_DIM), BF16),
        jax.ShapeDtypeStruct((T, HGRN_W), BF16),
        jax.ShapeDtypeStruct((T, HGRN_W), BF16),
        jax.ShapeDtypeStruct((T, HGRN_W), BF16),
        jax.ShapeDtypeStruct((T, HGRN_W), F32),
        jax.ShapeDtypeStruct((T, HGRN_W), BF16),
        jax.ShapeDtypeStruct((T, HGRN_W), F32),
        jax.ShapeDtypeStruct((T, HGRN_W), BF16),
        jax.ShapeDtypeStruct((T, D_MODEL), BF16),
        jax.ShapeDtypeStruct((T, D_MODEL), BF16),
    ]
    out_specs = [
        pl.BlockSpec((1, ATTN_HEADS, tm, HEAD_DIM), lambda i: (i // nsb, 0, i % nsb, 0)),
        pl.BlockSpec((1, ATTN_KV_HEADS, 1, HEAD_DIM, tm), lambda i: (i // nsb, 0, i % nsb, 0, 0)),
        pl.BlockSpec((1, ATTN_KV_HEADS, tm, HEAD_DIM), lambda i: (i // nsb, 0, i % nsb, 0)),
        tok(HGRN_W), tok(HGRN_W), tok(HGRN_W), tok(HGRN_W), tok(HGRN_W), tok(HGRN_W), tok(HGRN_W),
        tok(D_MODEL), tok(D_MODEL),
    ]
    return pl.pallas_call(
        _inproj_kernel, grid=(T // tm,), in_specs=in_specs, out_specs=out_specs, out_shape=out_shape,
        compiler_params=_params(("parallel",)), name="in_proj",
    )(x2d, g_mix, w_all, gq, gk, cq, sq, ckt, skt, lb_fwd, lb_bwd, e64)


def _attn_kernel(q_ref, kt_ref, v_ref, o_ref, m_ref, l_ref, acc_ref):
    tq = q_ref.shape[2]
    nk, tk = kt_ref.shape[2], kt_ref.shape[4]
    q = q_ref[0].reshape(GQA_GROUP * tq, HEAD_DIM)
    m_ref[...] = jnp.full(m_ref.shape, NEG_BIG, F32)
    l_ref[...] = jnp.zeros(l_ref.shape, F32)
    acc_ref[...] = jnp.zeros(acc_ref.shape, F32)

    def body(j, carry):
        kt = kt_ref[0, 0, j]
        vv = v_ref[0, 0, pl.ds(pl.multiple_of(j * tk, tk), tk), :]
        s = _dot(q, kt)
        m_prev = m_ref[...]
        m_new = jnp.maximum(m_prev, jnp.max(s, axis=1, keepdims=True))
        alpha = jnp.exp(m_prev - m_new)
        p = jnp.exp(s - m_new)
        l_ref[...] = alpha * l_ref[...] + jnp.sum(p, axis=1, keepdims=True)
        acc_ref[...] = alpha * acc_ref[...] + _dot(p.astype(BF16), vv)
        m_ref[...] = m_new
        return carry

    lax.fori_loop(0, nk, body, 0)
    o = (acc_ref[...] * (1.0 / l_ref[...])).reshape(GQA_GROUP, tq, HEAD_DIM)
    o_ref[0] = jnp.concatenate([o[i] for i in range(GQA_GROUP)], axis=1).astype(BF16)


def _attention(q, kt, v, *, B, S, tq):
    nk, tk = kt.shape[2], kt.shape[4]
    gw = GQA_GROUP * HEAD_DIM
    M = GQA_GROUP * tq
    return pl.pallas_call(
        _attn_kernel, grid=(B, ATTN_KV_HEADS, S // tq),
        in_specs=[
            pl.BlockSpec((1, GQA_GROUP, tq, HEAD_DIM), lambda b, g, i: (b, g, i, 0)),
            pl.BlockSpec((1, 1, nk, HEAD_DIM, tk), lambda b, g, i: (b, g, 0, 0, 0)),
            pl.BlockSpec((1, 1, S, HEAD_DIM), lambda b, g, i: (b, g, 0, 0)),
        ],
        out_specs=pl.BlockSpec((1, tq, gw), lambda b, g, i: (b, i, g)),
        out_shape=jax.ShapeDtypeStruct((B, S, ATTN_Q_W), BF16),
        scratch_shapes=[pltpu.VMEM((M, 1), F32), pltpu.VMEM((M, 1), F32), pltpu.VMEM((M, HEAD_DIM), F32)],
        compiler_params=_params(("parallel", "parallel", "parallel")), name="attn",
    )(q, kt, v)


def _hgrn_consts(cb, reverse):
    C = HGRN_CHUNK
    t = np.arange(cb)[:, None]
    s = np.arange(cb)[None, :]
    same = (t // C) == (s // C)
    if reverse:
        run = same & (s >= t)
        ref = same & ((s % C) >= C - 1 - C // 2)
    else:
        run = same & (s <= t)
        ref = same & ((s % C) <= C // 2)
    return jnp.asarray(np.concatenate([run, ref, same], axis=0), dtype=BF16)


def _hgrn_direction(q_ref, v_ref, k_ref, lf_ref, cm_ref, o_ref, st_ref, reverse):
    cb = q_ref.shape[0]
    C = HGRN_CHUNK
    n = cb // C
    row = lax.broadcasted_iota(jnp.int32, (cb, cb), 0)
    col = lax.broadcasted_iota(jnp.int32, (cb, cb), 1)
    same = (row // C) == (col // C)
    mask = same & ((col >= row) if reverse else (col <= row))
    cm = cm_ref[...]
    tn = (((0,), (0,)), ((), ()))
    nt = (((1,), (1,)), ((), ()))
    for hh in range(HGRN_HEADS):
        sl = slice(HGRN_DK * hh, HGRN_DK * (hh + 1))
        lf = lf_ref[:, sl]
        lf_hi = lf.astype(BF16)
        lf_lo = (lf - lf_hi.astype(F32)).astype(BF16)
        sums = _dot(cm, lf_hi) + _dot(cm, lf_lo)
        b, b_ref, b_last = sums[0:cb], sums[cb:2 * cb], sums[2 * cb:3 * cb]
        q = q_ref[:, sl].astype(F32)
        k = k_ref[:, sl].astype(F32)
        v = v_ref[:, sl]
        qs = (q * jnp.exp(b - b_ref)).astype(BF16)
        ks = (k * jnp.exp(b_ref - b)).astype(BF16)
        a = lax.dot_general(qs, ks, nt, preferred_element_type=F32)
        o_intra = _dot(jnp.where(mask, a, 0.0).astype(BF16), v)
        qi = (q * jnp.exp(b)).astype(BF16)
        kst = (k * jnp.exp(b_last - b)).astype(BF16)
        dec = jnp.exp(b_last)
        st = st_ref[hh]
        outs = [None] * n
        for ci in (range(n - 1, -1, -1) if reverse else range(n)):
            r = slice(C * ci, C * (ci + 1))
            outs[ci] = o_intra[r] + lax.dot_general(qi[r], st.astype(BF16), nt, preferred_element_type=F32)
            upd = lax.dot_general(v[r], kst[r], tn, preferred_element_type=F32)
            st = dec[C * ci:C * ci + 1, :] * st + upd
        st_ref[hh] = st
        o_ref[:, sl] = jnp.concatenate(outs, axis=0)


def _hgrn_kernel(qf_ref, vf_ref, kf_ref, lff_ref, qb_ref, vb_ref, kb_ref, lfb_ref, cmf_ref, cmb_ref,
                 of_ref, ob_ref, stf_ref, stb_ref):
    @pl.when(pl.program_id(1) == 0)
    def _():
        stf_ref[...] = jnp.zeros(stf_ref.shape, F32)
        stb_ref[...] = jnp.zeros(stb_ref.shape, F32)

    _hgrn_direction(qf_ref, vf_ref, kf_ref, lff_ref, cmf_ref, of_ref, stf_ref, False)
    _hgrn_direction(qb_ref, vb_ref, kb_ref, lfb_ref, cmb_ref, ob_ref, stb_ref, True)


def _hgrn(hq, hv, kf, lff, kb, lfb, *, B, S, cb):
    T = B * S
    nblk = S // cb
    fwd = pl.BlockSpec((cb, HGRN_W), lambda b, i: (b * nblk + i, 0))
    bwd = pl.BlockSpec((cb, HGRN_W), lambda b, i: (b * nblk + nblk - 1 - i, 0))
    cm = _const_spec((3 * cb, cb))
    st = pltpu.VMEM((HGRN_HEADS, HGRN_DK, HGRN_DK), F32)
    return pl.pallas_call(
        _hgrn_kernel, grid=(B, nblk),
        in_specs=[fwd, fwd, fwd, fwd, bwd, bwd, bwd, bwd, cm, cm],
        out_specs=[fwd, bwd],
        out_shape=[jax.ShapeDtypeStruct((T, HGRN_W), F32)] * 2,
        scratch_shapes=[st, st],
        compiler_params=_params(("parallel", "arbitrary")), name="hgrn",
    )(hq, hv, kf, lff, hq, hv, kb, lfb, _hgrn_consts(cb, False), _hgrn_consts(cb, True))


def _merge_kernel(ao_ref, of_ref, ob_ref, og_ref, sa_ref, sb_ref, x_ref, wa_ref, wb_ref, wo_ref,
                  gh_ref, gf_ref, wr_ref, x1_ref, h2_ref, lg_ref):
    o = of_ref[...] + ob_ref[...]
    parts = []
    for hh in range(HGRN_HEADS):
        oh = o[:, HGRN_DK * hh:HGRN_DK * (hh + 1)]
        parts.append(oh * lax.rsqrt(jnp.mean(oh * oh, axis=1, keepdims=True) + EPS))
    hn = jnp.concatenate(parts, axis=1) * gh_ref[...] * og_ref[...].astype(F32)
    merged = (sa_ref[...].astype(F32) * _dot(ao_ref[...], wa_ref[...])
              + sb_ref[...].astype(F32) * _dot(hn.astype(BF16), wb_ref[...]))
    x1 = x_ref[...] + _dot(merged.astype(BF16), wo_ref[...])
    x1_ref[...] = x1
    h2 = x1 * lax.rsqrt(jnp.mean(x1 * x1, axis=1, keepdims=True) + EPS) * gf_ref[...]
    hi = h2.astype(BF16)
    lo = (h2 - hi.astype(F32)).astype(BF16)
    h2_ref[...] = hi
    lg_ref[...] = _dot(hi, wr_ref[0]) + _dot(lo, wr_ref[0]) + _dot(hi, wr_ref[1])


def _merge(ao, of, ob, og, sa, sb, x2d, wa, wb, wo, gh, gf, wr, *, tm):
    T = x2d.shape[0]
    tok = lambda w: pl.BlockSpec((tm, w), lambda i: (i, 0))
    return pl.pallas_call(
        _merge_kernel, grid=(T // tm,),
        in_specs=[tok(ATTN_Q_W), tok(HGRN_W), tok(HGRN_W), tok(HGRN_W), tok(D_MODEL), tok(D_MODEL), tok(D_MODEL),
                  _const_spec(wa.shape), _const_spec(wb.shape), _const_spec(wo.shape),
                  _const_spec(gh.shape), _const_spec(gf.shape), _const_spec(wr.shape)],
        out_specs=[tok(D_MODEL), tok(D_MODEL), tok(ROUTE_W)],
        out_shape=[jax.ShapeDtypeStruct((T, D_MODEL), F32), jax.ShapeDtypeStruct((T, D_MODEL), BF16),
                   jax.ShapeDtypeStruct((T, ROUTE_W), F32)],
        compiler_params=_params(("parallel",)), name="merge",
    )(ao, of, ob, og, sa, sb, x2d, wa, wb, wo, gh, gf, wr)


def _first_argmax(vals, n):
    top = jnp.max(vals, axis=0, keepdims=True)
    rows = lax.broadcasted_iota(jnp.int32, vals.shape, 0)
    idx = jnp.min(jnp.where(vals == top, rows, n), axis=0, keepdims=True)
    return top, idx, rows


def _route_kernel(lg_ref, bias_ref, dw_ref):
    lt = lg_ref[...].T + bias_ref[...]
    gl = lt[0:N_GROUPS]
    ge = jnp.exp(gl - jnp.max(gl, axis=0, keepdims=True))
    gp = ge / jnp.sum(ge, axis=0, keepdims=True)
    p_g, g_idx, _ = _first_argmax(gp, N_GROUPS)
    esel = lt[EXPERT_ROW0:EXPERT_ROW0 + EPG]
    for gg in range(1, N_GROUPS):
        esel = jnp.where(g_idx == gg, lt[EXPERT_ROW0 + EPG * gg:EXPERT_ROW0 + EPG * (gg + 1)], esel)
    ee = jnp.exp(esel - jnp.max(esel, axis=0, keepdims=True))
    pe = ee / jnp.sum(ee, axis=0, keepdims=True)
    p1, i1, rows = _first_argmax(pe, EPG)
    p2, i2, _ = _first_argmax(jnp.where(rows == i1, -1.0, pe), EPG)
    den = p1 + p2
    c1 = p_g * (p1 / den)
    c2 = p_g * (p2 / den)
    e1 = g_idx * EPG + i1
    e2 = g_idx * EPG + i2
    er = lax.broadcasted_iota(jnp.int32, lt.shape, 0)
    dw_ref[...] = (jnp.where(er == e1, c1, 0.0) + jnp.where(er == e2, c2, 0.0)).T


def _route(lg, bias, *, tm):
    T = lg.shape[0]
    return pl.pallas_call(
        _route_kernel, grid=(T // tm,),
        in_specs=[pl.BlockSpec((tm, ROUTE_W), lambda i: (i, 0)), _const_spec((ROUTE_W, 1))],
        out_specs=pl.BlockSpec((tm, ROUTE_W), lambda i: (i, 0)),
        out_shape=jax.ShapeDtypeStruct((T, ROUTE_W), F32),
        compiler_params=_params(("parallel",)), name="route",
    )(lg, bias)


def _moe_kernel(h_ref, dw_ref, x1_ref, wg_ref, wu_ref, wd_ref, o_ref):
    e = pl.program_id(1)

    @pl.when(e == 0)
    def _():
        o_ref[...] = x1_ref[...]

    h = h_ref[...]
    lane = lax.broadcasted_iota(jnp.int32, dw_ref.shape, 1)
    w = jnp.sum(jnp.where(lane == e, dw_ref[...], 0.0), axis=1, keepdims=True)
    hid = _silu(_dot(h, wg_ref[0])) * _dot(h, wu_ref[0]) * w
    o_ref[...] += _dot(hid.astype(BF16), wd_ref[0])


def _moe(h2, dw, x1, wg, wu, wd, *, tm):
    T = h2.shape[0]
    tok = lambda w: pl.BlockSpec((tm, w), lambda i, e: (i, 0))
    return pl.pallas_call(
        _moe_kernel, grid=(T // tm, N_EXPERTS),
        in_specs=[tok(D_MODEL), tok(ROUTE_W), tok(D_MODEL),
                  pl.BlockSpec((1, D_MODEL, D_EXPERT), lambda i, e: (e, 0, 0)),
                  pl.BlockSpec((1, D_MODEL, D_EXPERT), lambda i, e: (e, 0, 0)),
                  pl.BlockSpec((1, D_EXPERT, D_MODEL), lambda i, e: (e, 0, 0))],
        out_specs=tok(D_MODEL),
        out_shape=jax.ShapeDtypeStruct((T, D_MODEL), F32),
        compiler_params=_params(("parallel", "arbitrary")), name="moe",
    )(h2, dw, x1, wg, wu, wd)


def _tiles(S):
    return dict(tm=min(512, S), tq=min(256, S), cb=min(256, S), tmerge=min(512, S), troute=min(1024, S),
                tmoe=min(1024, S))


def _rope_tables(S):
    rows = S // GRID_W
    row_ids = jnp.repeat(jnp.arange(rows), GRID_W).astype(F32)
    col_ids = jnp.tile(jnp.arange(GRID_W), rows).astype(F32)
    inv_freq = ROPE_THETA ** (-jnp.arange(0, HALF, 2, dtype=F32) / HALF)
    ang = jnp.concatenate([row_ids[:, None] * inv_freq, col_ids[:, None] * inv_freq], axis=-1)
    return jnp.cos(ang), jnp.sin(ang)


def kernel(x, g_mix, w_in, q_norm, k_norm, hgrn_norm, lb_fwd, lb_bwd, w_attn_branch, w_hgrn_branch, w_out, g_ffn,
           w_router_group, b_router_group, w_router_expert, b_router_expert, w_exp_gate, w_exp_up, w_exp_down):
    B, S, D = x.shape
    assert D == D_MODEL and w_in.shape == (1, D_MODEL, D_IN) and lb_fwd.shape[0] == 2
    T = B * S
    t = _tiles(S)

    perm = np.concatenate([np.arange(0, HEAD_DIM, 2), np.arange(1, HEAD_DIM, 2)])
    qcols = (np.arange(ATTN_HEADS)[:, None] * HEAD_DIM + perm[None, :]).reshape(-1)
    kcols = IN_OFF[1] + (np.arange(ATTN_KV_HEADS)[:, None] * HEAD_DIM + perm[None, :]).reshape(-1)
    cols = np.concatenate([qcols, kcols, np.arange(IN_OFF[2], D_IN)])
    w_all = w_in[0][:, cols].astype(BF16)
    gq = jnp.tile(q_norm[0][perm], ATTN_HEADS)[None, :].astype(F32)
    gk = k_norm[0][perm][:, None].astype(F32)
    cos, sin = _rope_tables(S)
    heads_per_tile = LANES // HEAD_DIM
    cq = jnp.tile(jnp.concatenate([cos, cos], axis=1), (1, heads_per_tile))
    sq = jnp.tile(jnp.concatenate([-sin, sin], axis=1), (1, heads_per_tile))
    hid = np.arange(ATTN_Q_W) // HEAD_DIM
    e64 = jnp.asarray(hid[:, None] == hid[None, :], dtype=BF16)

    (q, kt, v, hq, hv, kf, lff, kb, lfb, og, sa, sb) = _in_proj(
        x.reshape(T, D), g_mix.astype(F32), w_all, gq, gk, cq, sq, cos.T, sin.T,
        lb_fwd.astype(F32), lb_bwd.astype(F32), e64, B=B, S=S, tm=t["tm"])

    ao = _attention(q, kt, v, B=B, S=S, tq=t["tq"]).reshape(T, ATTN_Q_W)
    of, ob = _hgrn(hq, hv, kf, lff, kb, lfb, B=B, S=S, cb=t["cb"])

    wr = jnp.zeros((D_MODEL, ROUTE_W), F32)
    wr = wr.at[:, 0:N_GROUPS].set(w_router_group[0]).at[:, EXPERT_ROW0:EXPERT_ROW0 + N_EXPERTS].set(w_router_expert[0])
    wr_hi = wr.astype(BF16)
    wr_lo = (wr - wr_hi.astype(F32)).astype(BF16)
    rbias = jnp.zeros((ROUTE_W, 1), F32)
    rbias = rbias.at[0:N_GROUPS, 0].set(b_router_group[0]).at[EXPERT_ROW0:EXPERT_ROW0 + N_EXPERTS, 0].set(
        b_router_expert[0])
    x1, h2, lg = _merge(
        ao, of, ob, og, sa, sb, x.reshape(T, D), w_attn_branch[0].astype(BF16), w_hgrn_branch[0].astype(BF16),
        w_out[0].astype(BF16), jnp.tile(hgrn_norm[0], HGRN_HEADS)[None, :].astype(F32), g_ffn.astype(F32),
        jnp.stack([wr_hi, wr_lo]), tm=t["tmerge"])
    dw = _route(lg, rbias, tm=t["troute"])
    out = _moe(h2, dw, x1, w_exp_gate[0].astype(BF16), w_exp_up[0].astype(BF16), w_exp_down[0].astype(BF16),
               tm=t["tmoe"])
    return out.reshape(B, S, D)
```

```python
import functools
import math

import jax
import jax.numpy as jnp
import numpy as np
from jax import lax
from jax.experimental import pallas as pl
from jax.experimental.pallas import tpu as pltpu

F32 = jnp.float32
BF16 = jnp.bfloat16

D_MODEL = 1024
GRID_W = 64
EPS = 1e-6
ATTN_HEADS = 8
ATTN_KV_HEADS = 2
GQA_GROUP = ATTN_HEADS // ATTN_KV_HEADS
HEAD_DIM = 64
HALF = HEAD_DIM // 2
ROPE_THETA = 10000.0
HGRN_HEADS = 4
HGRN_DK = 128
HGRN_CHUNK = 32
HGRN_SCALE = HGRN_DK ** -0.5
N_GROUPS = 4
EPG = 8
N_EXPERTS = N_GROUPS * EPG
TOP_K = 2
D_EXPERT = 512
ATTN_Q_W = ATTN_HEADS * HEAD_DIM
ATTN_KV_W = ATTN_KV_HEADS * HEAD_DIM
HGRN_W = HGRN_HEADS * HGRN_DK
IN_SPLITS = (ATTN_Q_W, ATTN_KV_W, ATTN_KV_W, HGRN_W, HGRN_W, HGRN_W, HGRN_W, HGRN_W, D_MODEL, D_MODEL)
IN_OFF = tuple(int(v) for v in np.cumsum((0,) + IN_SPLITS))
D_IN = IN_OFF[-1]
LANES = 128
ROUTE_W = LANES
EXPERT_ROW0 = 8
VMEM_LIMIT = 56 * 1024 * 1024
NEG_BIG = -1e30
LOG2E = math.log2(math.e)
V_ROWS = HEAD_DIM + 16
SAFE_SCORE_BOUND = 60.0


def _sigmoid(x):
    return 1.0 / (1.0 + jnp.exp(-x))


def _silu(x):
    return x * _sigmoid(x)


def _dot(a, b):
    return jnp.dot(a, b, preferred_element_type=F32)


def _params(sem):
    return pltpu.CompilerParams(dimension_semantics=sem, vmem_limit_bytes=VMEM_LIMIT)


def _const_spec(shape):
    nd = len(shape)
    return pl.BlockSpec(shape, lambda *_: (0,) * nd)


def _inproj_kernel(x_ref, g_ref, w_ref, gq_ref, gk_ref, cos_ref, sin_ref, lbf_ref, lbb_ref,
                   qt_ref, k_ref, kn2_ref, vt_ref, hq_ref, hv_ref, kf_ref, lff_ref, kb_ref, lfb_ref,
                   og_ref, sa_ref, sb_ref):
    tm = x_ref.shape[0]
    x = x_ref[...]
    ms = jnp.mean(x * x, axis=-1, keepdims=True)
    h = (x * lax.rsqrt(ms + EPS) * g_ref[...]).astype(BF16)

    def proj(i):
        return _dot(h, w_ref[:, IN_OFF[i]:IN_OFF[i + 1]])

    def norm_rope_t(y, n_heads, gain_ref):
        yt = y.T.reshape(n_heads, HEAD_DIM, tm)
        ss = jnp.sum(yt * yt, axis=1, keepdims=True)
        yn = yt * lax.rsqrt(ss * (1.0 / HEAD_DIM) + EPS) * gain_ref[...][None]
        a, b = yn[:, :HALF, :], yn[:, HALF:, :]
        c, s = cos_ref[...][None], sin_ref[...][None]
        return jnp.concatenate([a * c - b * s, a * s + b * c], axis=1)

    qt_ref[0] = (norm_rope_t(proj(0), ATTN_HEADS, gq_ref) * (HEAD_DIM ** -0.5 * LOG2E)).astype(BF16)
    kt = norm_rope_t(proj(1), ATTN_KV_HEADS, gk_ref).astype(BF16).astype(F32)
    kn2_ref[0] = jnp.sum(kt * kt, axis=1)
    k = kt.reshape(ATTN_KV_W, tm).T
    for gg in range(ATTN_KV_HEADS):
        k_ref[0, gg] = k[:, HEAD_DIM * gg:HEAD_DIM * (gg + 1)].astype(BF16)
    vt = proj(2).T.reshape(ATTN_KV_HEADS, HEAD_DIM, tm)
    ones = jnp.ones((ATTN_KV_HEADS, V_ROWS - HEAD_DIM, tm), F32)
    vt_ref[0, :, 0] = jnp.concatenate([vt, ones], axis=1).astype(BF16)

    hq_ref[...] = (_silu(proj(3)) * HGRN_SCALE).astype(BF16)

    def direction(z, lbraw_ref, k_out, lf_out):
        raw = lbraw_ref[...]
        e = jnp.exp(raw - jnp.max(raw, axis=0, keepdims=True))
        lb = e[0:1] / jnp.sum(e, axis=0, keepdims=True)
        k_out[...] = ((1.0 - lb) * _sigmoid(-z)).astype(BF16)
        lf_out[...] = jnp.log(lb + (1.0 - lb) * _sigmoid(z))

    direction(proj(4), lbf_ref, kf_ref, lff_ref)
    direction(proj(5), lbb_ref, kb_ref, lfb_ref)
    hv_ref[...] = proj(6).astype(BF16)
    og_ref[...] = _silu(proj(7)).astype(BF16)
    sa_ref[...] = _sigmoid(proj(8)).astype(BF16)
    sb_ref[...] = _sigmoid(proj(9)).astype(BF16)


def _in_proj(x2d, g_mix, w_all, gq, gk, cos_t, sin_t, lb_fwd, lb_bwd, *, B, S, tm):
    T = B * S
    nsb = S // tm
    tok = lambda w: pl.BlockSpec((tm, w), lambda i: (i, 0))
    in_specs = [
        tok(D_MODEL), _const_spec((1, D_MODEL)), _const_spec((D_MODEL, D_IN)),
        _const_spec((HEAD_DIM, 1)), _const_spec((HEAD_DIM, 1)),
        pl.BlockSpec((HALF, tm), lambda i: (0, i % nsb)), pl.BlockSpec((HALF, tm), lambda i: (0, i % nsb)),
        _const_spec(lb_fwd.shape), _const_spec(lb_bwd.shape),
    ]
    out_shape = [
        jax.ShapeDtypeStruct((B, ATTN_HEADS, HEAD_DIM, S), BF16),
        jax.ShapeDtypeStruct((B, ATTN_KV_HEADS, S, HEAD_DIM), BF16),
        jax.ShapeDtypeStruct((B, ATTN_KV_HEADS, S), F32),
        jax.ShapeDtypeStruct((B, ATTN_KV_HEADS, nsb, V_ROWS, tm), BF16),
        jax.ShapeDtypeStruct((T, HGRN_W), BF16),
        jax.ShapeDtypeStruct((T, HGRN_W), BF16),
        jax.ShapeDtypeStruct((T, HGRN_W), BF16),
        jax.ShapeDtypeStruct((T, HGRN_W), F32),
        jax.ShapeDtypeStruct((T, HGRN_W), BF16),
        jax.ShapeDtypeStruct((T, HGRN_W), F32),
        jax.ShapeDtypeStruct((T, HGRN_W), BF16),
        jax.ShapeDtypeStruct((T, D_MODEL), BF16),
        jax.ShapeDtypeStruct((T, D_MODEL), BF16),
    ]
    out_specs = [
        pl.BlockSpec((1, ATTN_HEADS, HEAD_DIM, tm), lambda i: (i // nsb, 0, 0, i % nsb)),
        pl.BlockSpec((1, ATTN_KV_HEADS, tm, HEAD_DIM), lambda i: (i // nsb, 0, i % nsb, 0)),
        pl.BlockSpec((1, ATTN_KV_HEADS, tm), lambda i: (i // nsb, 0, i % nsb)),
        pl.BlockSpec((1, ATTN_KV_HEADS, 1, V_ROWS, tm), lambda i: (i // nsb, 0, i % nsb, 0, 0)),
        tok(HGRN_W), tok(HGRN_W), tok(HGRN_W), tok(HGRN_W), tok(HGRN_W), tok(HGRN_W), tok(HGRN_W),
        tok(D_MODEL), tok(D_MODEL),
    ]
    return pl.pallas_call(
        _inproj_kernel, grid=(T // tm,), in_specs=in_specs, out_specs=out_specs, out_shape=out_shape,
        compiler_params=_params(("parallel",)), name="in_proj",
    )(x2d, g_mix, w_all, gq, gk, cos_t, sin_t, lb_fwd, lb_bwd)


def _attn_kernel(qt_ref, k_ref, kn2_ref, vt_ref, o_ref, shift_ref, acc_ref):
    g = pl.program_id(1)
    nk, tk = vt_ref.shape[2], vt_ref.shape[4]

    def keys(j):
        return k_ref[0, 0, pl.ds(pl.multiple_of(j * tk, tk), tk), :]

    kmax2 = jnp.max(kn2_ref[0, pl.ds(g, 1), :], axis=1, keepdims=True)
    for hh in range(GQA_GROUP):
        q = qt_ref[0, hh].astype(F32)
        shift_ref[hh] = jnp.sqrt(jnp.sum(q * q, axis=0, keepdims=True) * kmax2)

    @pl.when(jnp.max(shift_ref[...]) > SAFE_SCORE_BOUND)
    def _():
        shift_ref[...] = jnp.full(shift_ref.shape, NEG_BIG, F32)

        def max_body(j, carry):
            k = keys(j)
            for hh in range(GQA_GROUP):
                st = _dot(k, qt_ref[0, hh])
                shift_ref[hh] = jnp.maximum(shift_ref[hh], jnp.max(st, axis=0, keepdims=True))
            return carry

        lax.fori_loop(0, nk, max_body, 0)

    acc_ref[...] = jnp.zeros(acc_ref.shape, F32)

    def body(j, carry):
        k = keys(j)
        vt = vt_ref[0, 0, j]
        sts = [_dot(k, qt_ref[0, hh]) for hh in range(GQA_GROUP)]
        for hh in range(GQA_GROUP):
            pt = jnp.exp2(sts[hh] - shift_ref[hh])
            acc_ref[hh] += _dot(vt, pt.astype(BF16))
        return carry

    lax.fori_loop(0, nk, body, 0)
    outs = []
    for hh in range(GQA_GROUP):
        acc = acc_ref[hh]
        outs.append(acc[:HEAD_DIM] * (1.0 / acc[HEAD_DIM:HEAD_DIM + 1]))
    o_ref[0] = jnp.concatenate(outs, axis=0).T.astype(BF16)


def _attention(qt, k, kn2, vt, *, B, S, tq):
    nk, tk = vt.shape[2], vt.shape[4]
    gw = GQA_GROUP * HEAD_DIM
    return pl.pallas_call(
        _attn_kernel, grid=(B, ATTN_KV_HEADS, S // tq),
        in_specs=[
            pl.BlockSpec((1, GQA_GROUP, HEAD_DIM, tq), lambda b, g, i: (b, g, 0, i)),
            pl.BlockSpec((1, 1, S, HEAD_DIM), lambda b, g, i: (b, g, 0, 0)),
            pl.BlockSpec((1, ATTN_KV_HEADS, S), lambda b, g, i: (b, 0, 0)),
            pl.BlockSpec((1, 1, nk, V_ROWS, tk), lambda b, g, i: (b, g, 0, 0, 0)),
        ],
        out_specs=pl.BlockSpec((1, tq, gw), lambda b, g, i: (b, i, g)),
        out_shape=jax.ShapeDtypeStruct((B, S, ATTN_Q_W), BF16),
        scratch_shapes=[pltpu.VMEM((GQA_GROUP, 1, tq), F32), pltpu.VMEM((GQA_GROUP, V_ROWS, tq), F32)],
        compiler_params=_params(("parallel", "parallel", "parallel")), name="attn",
    )(qt, k, kn2, vt)


def _hgrn_consts(cb, reverse):
    C = HGRN_CHUNK
    t = np.arange(cb)[:, None]
    s = np.arange(cb)[None, :]
    same = (t // C) == (s // C)
    if reverse:
        run = same & (s >= t)
        ref = same & ((s % C) >= C - 1 - C // 2)
    else:
        run = same & (s <= t)
        ref = same & ((s % C) <= C // 2)
    return jnp.asarray(np.concatenate([run, ref, same], axis=0), dtype=BF16)


def _hgrn_direction(q_ref, v_ref, k_ref, lf_ref, cm_ref, o_ref, st_ref, reverse):
    cb = q_ref.shape[0]
    C = HGRN_CHUNK
    n = cb // C
    row = lax.broadcasted_iota(jnp.int32, (cb, cb), 0)
    col = lax.broadcasted_iota(jnp.int32, (cb, cb), 1)
    same = (row // C) == (col // C)
    mask = same & ((col >= row) if reverse else (col <= row))
    cm = cm_ref[...]
    tn = (((0,), (0,)), ((), ()))
    nt = (((1,), (1,)), ((), ()))
    for hh in range(HGRN_HEADS):
        sl = slice(HGRN_DK * hh, HGRN_DK * (hh + 1))
        lf = lf_ref[:, sl]
        lf_hi = lf.astype(BF16)
        lf_lo = (lf - lf_hi.astype(F32)).astype(BF16)
        sums = _dot(cm, lf_hi) + _dot(cm, lf_lo)
        b, b_ref, b_last = sums[0:cb], sums[cb:2 * cb], sums[2 * cb:3 * cb]
        q = q_ref[:, sl].astype(F32)
        k = k_ref[:, sl].astype(F32)
        v = v_ref[:, sl]
        qs = (q * jnp.exp(b - b_ref)).astype(BF16)
        ks = (k * jnp.exp(b_ref - b)).astype(BF16)
        a = lax.dot_general(qs, ks, nt, preferred_element_type=F32)
        o_intra = _dot(jnp.where(mask, a, 0.0).astype(BF16), v)
        qi = (q * jnp.exp(b)).astype(BF16)
        kst = (k * jnp.exp(b_last - b)).astype(BF16)
        dec = jnp.exp(b_last)
        st = st_ref[hh]
        outs = [None] * n
        for ci in (range(n - 1, -1, -1) if reverse else range(n)):
            r = slice(C * ci, C * (ci + 1))
            outs[ci] = o_intra[r] + lax.dot_general(qi[r], st.astype(BF16), nt, preferred_element_type=F32)
            upd = lax.dot_general(v[r], kst[r], tn, preferred_element_type=F32)
            st = dec[C * ci:C * ci + 1, :] * st + upd
        st_ref[hh] = st
        o_ref[:, sl] = jnp.concatenate(outs, axis=0)


def _hgrn_kernel(qf_ref, vf_ref, kf_ref, lff_ref, qb_ref, vb_ref, kb_ref, lfb_ref, cmf_ref, cmb_ref,
                 of_ref, ob_ref, stf_ref, stb_ref):
    @pl.when(pl.program_id(1) == 0)
    def _():
        stf_ref[...] = jnp.zeros(stf_ref.shape, F32)
        stb_ref[...] = jnp.zeros(stb_ref.shape, F32)

    _hgrn_direction(qf_ref, vf_ref, kf_ref, lff_ref, cmf_ref, of_ref, stf_ref, False)
    _hgrn_direction(qb_ref, vb_ref, kb_ref, lfb_ref, cmb_ref, ob_ref, stb_ref, True)


def _hgrn(hq, hv, kf, lff, kb, lfb, *, B, S, cb):
    T = B * S
    nblk = S // cb
    fwd = pl.BlockSpec((cb, HGRN_W), lambda b, i: (b * nblk + i, 0))
    bwd = pl.BlockSpec((cb, HGRN_W), lambda b, i: (b * nblk + nblk - 1 - i, 0))
    cm = _const_spec((3 * cb, cb))
    st = pltpu.VMEM((HGRN_HEADS, HGRN_DK, HGRN_DK), F32)
    return pl.pallas_call(
        _hgrn_kernel, grid=(B, nblk),
        in_specs=[fwd, fwd, fwd, fwd, bwd, bwd, bwd, bwd, cm, cm],
        out_specs=[fwd, bwd],
        out_shape=[jax.ShapeDtypeStruct((T, HGRN_W), F32)] * 2,
        scratch_shapes=[st, st],
        compiler_params=_params(("parallel", "arbitrary")), name="hgrn",
    )(hq, hv, kf, lff, hq, hv, kb, lfb, _hgrn_consts(cb, False), _hgrn_consts(cb, True))


def _merge_kernel(ao_ref, of_ref, ob_ref, og_ref, sa_ref, sb_ref, x_ref, wa_ref, wb_ref, wo_ref,
                  gh_ref, gf_ref, wr_ref, x1_ref, h2_ref, lg_ref):
    o = of_ref[...] + ob_ref[...]
    parts = []
    for hh in range(HGRN_HEADS):
        oh = o[:, HGRN_DK * hh:HGRN_DK * (hh + 1)]
        parts.append(oh * lax.rsqrt(jnp.mean(oh * oh, axis=1, keepdims=True) + EPS))
    hn = jnp.concatenate(parts, axis=1) * gh_ref[...] * og_ref[...].astype(F32)
    merged = (sa_ref[...].astype(F32) * _dot(ao_ref[...], wa_ref[...])
              + sb_ref[...].astype(F32) * _dot(hn.astype(BF16), wb_ref[...]))
    x1 = x_ref[...] + _dot(merged.astype(BF16), wo_ref[...])
    x1_ref[...] = x1
    h2 = x1 * lax.rsqrt(jnp.mean(x1 * x1, axis=1, keepdims=True) + EPS) * gf_ref[...]
    hi = h2.astype(BF16)
    lo = (h2 - hi.astype(F32)).astype(BF16)
    h2_ref[...] = h2
    lg_ref[...] = _dot(hi, wr_ref[0]) + _dot(lo, wr_ref[0]) + _dot(hi, wr_ref[1])


def _merge(ao, of, ob, og, sa, sb, x2d, wa, wb, wo, gh, gf, wr, *, tm):
    T = x2d.shape[0]
    tok = lambda w: pl.BlockSpec((tm, w), lambda i: (i, 0))
    return pl.pallas_call(
        _merge_kernel, grid=(T // tm,),
        in_specs=[tok(ATTN_Q_W), tok(HGRN_W), tok(HGRN_W), tok(HGRN_W), tok(D_MODEL), tok(D_MODEL), tok(D_MODEL),
                  _const_spec(wa.shape), _const_spec(wb.shape), _const_spec(wo.shape),
                  _const_spec(gh.shape), _const_spec(gf.shape), _const_spec(wr.shape)],
        out_specs=[tok(D_MODEL), tok(D_MODEL), tok(ROUTE_W)],
        out_shape=[jax.ShapeDtypeStruct((T, D_MODEL), F32), jax.ShapeDtypeStruct((T, D_MODEL), F32),
                   jax.ShapeDtypeStruct((T, ROUTE_W), F32)],
        compiler_params=_params(("parallel",)), name="merge",
    )(ao, of, ob, og, sa, sb, x2d, wa, wb, wo, gh, gf, wr)


def _first_argmax(vals, n):
    top = jnp.max(vals, axis=0, keepdims=True)
    rows = lax.broadcasted_iota(jnp.int32, vals.shape, 0)
    idx = jnp.min(jnp.where(vals == top, rows, n), axis=0, keepdims=True)
    return top, idx, rows


def _route_kernel(lg_ref, bias_ref, ids_ref, c1_ref, c2_ref):
    lt = lg_ref[...].T + bias_ref[...]
    gl = lt[0:N_GROUPS]
    ge = jnp.exp(gl - jnp.max(gl, axis=0, keepdims=True))
    gp = ge / jnp.sum(ge, axis=0, keepdims=True)
    p_g, g_idx, _ = _first_argmax(gp, N_GROUPS)
    esel = lt[EXPERT_ROW0:EXPERT_ROW0 + EPG]
    for gg in range(1, N_GROUPS):
        esel = jnp.where(g_idx == gg, lt[EXPERT_ROW0 + EPG * gg:EXPERT_ROW0 + EPG * (gg + 1)], esel)
    ee = jnp.exp(esel - jnp.max(esel, axis=0, keepdims=True))
    pe = ee / jnp.sum(ee, axis=0, keepdims=True)
    p1, i1, rows = _first_argmax(pe, EPG)
    p2, i2, _ = _first_argmax(jnp.where(rows == i1, -1.0, pe), EPG)
    den = p1 + p2
    c1 = p_g * (p1 / den)
    c2 = p_g * (p2 / den)
    e1 = g_idx * EPG + i1
    e2 = g_idx * EPG + i2
    tm = lt.shape[1]
    ids_ref[...] = jnp.concatenate([e1, e2, jnp.zeros((6, tm), jnp.int32)], axis=0)
    c1_ref[...] = jnp.broadcast_to(c1, (LANES, tm)).T
    c2_ref[...] = jnp.broadcast_to(c2, (LANES, tm)).T


def _route(lg, bias, *, tm):
    T = lg.shape[0]
    tok = pl.BlockSpec((tm, LANES), lambda i: (i, 0))
    return pl.pallas_call(
        _route_kernel, grid=(T // tm,),
        in_specs=[pl.BlockSpec((tm, ROUTE_W), lambda i: (i, 0)), _const_spec((ROUTE_W, 1))],
        out_specs=[pl.BlockSpec((8, tm), lambda i: (0, i)), tok, tok],
        out_shape=[jax.ShapeDtypeStruct((8, T), jnp.int32), jax.ShapeDtypeStruct((T, LANES), F32),
                   jax.ShapeDtypeStruct((T, LANES), F32)],
        compiler_params=_params(("parallel",)), name="route",
    )(lg, bias)


def _routing_tables(ids, tmx):
    T = ids.shape[1]
    P = TOP_K * T
    ef = ids.reshape(P)
    onehot = (ef[:, None] == jnp.arange(N_EXPERTS, dtype=jnp.int32)[None, :]).astype(jnp.int32)
    csum = jnp.cumsum(onehot, axis=0)
    cnt = csum[-1]
    rank = jnp.take_along_axis(csum, ef[:, None], axis=1)[:, 0] - 1
    padded = ((cnt + tmx - 1) // tmx) * tmx
    off = jnp.cumsum(padded) - padded
    start = jnp.cumsum(cnt) - cnt
    pos = (off[ef] + rank).astype(jnp.int32)
    order = jnp.argsort(ef, stable=True).astype(jnp.int32)
    n_rows = P + N_EXPERTS * tmx
    n_tiles = n_rows // tmx
    tile_e = jnp.searchsorted(off + padded, jnp.arange(n_tiles, dtype=jnp.int32) * tmx, side="right").astype(jnp.int32)
    tile_valid = (tile_e < N_EXPERTS).astype(jnp.int32)
    tile_e = jnp.minimum(tile_e, N_EXPERTS - 1)
    row_e = jnp.repeat(tile_e, tmx)
    r_in = jnp.arange(n_rows, dtype=jnp.int32) - off[row_e].astype(jnp.int32)
    real = (jnp.repeat(tile_valid, tmx) == 1) & (r_in < cnt[row_e])
    src = order[jnp.clip(start[row_e] + r_in, 0, P - 1)] % T
    tok = jnp.where(real, src, 0).astype(jnp.int32)
    return tok.reshape(n_tiles, 1, tmx), pos.reshape(TOP_K, T), tile_e, tile_valid


def _start_row_gather(idx_ref, src_hbm, dst, sem, n_rows):
    def issue(r, carry):
        pltpu.make_async_copy(src_hbm.at[pl.ds(idx_ref[0, 0, r], 1), :], dst.at[pl.ds(r, 1), :], sem).start()
        return carry

    lax.fori_loop(0, n_rows, issue, 0, unroll=8)


def _wait_row_gather(src_hbm, dst, sem, n_rows):
    pltpu.make_async_copy(src_hbm.at[pl.ds(0, n_rows), :], dst, sem).wait()


def _moe_kernel(te_ref, tv_ref, tok_cur_ref, tok_nxt_ref, h_hbm, wg_ref, wu_ref, wd_ref, y_ref, xbuf, sem):
    i = pl.program_id(0)
    n = pl.num_programs(0)
    tmx = xbuf.shape[1]
    slot = i % 2

    @pl.when((i == 0) & (tv_ref[0] == 1))
    def _():
        _start_row_gather(tok_cur_ref, h_hbm, xbuf.at[0], sem.at[0], tmx)

    @pl.when((i + 1 < n) & (tv_ref[jnp.minimum(i + 1, n - 1)] == 1))
    def _():
        _start_row_gather(tok_nxt_ref, h_hbm, xbuf.at[1 - slot], sem.at[1 - slot], tmx)

    @pl.when(tv_ref[i] == 1)
    def _():
        _wait_row_gather(h_hbm, xbuf.at[slot], sem.at[slot], tmx)
        x = xbuf[slot].astype(BF16)
        hid = _silu(_dot(x, wg_ref[0])) * _dot(x, wu_ref[0])
        y_ref[...] = _dot(hid.astype(BF16), wd_ref[0])

    @pl.when(tv_ref[i] == 0)
    def _():
        y_ref[...] = jnp.zeros(y_ref.shape, F32)


def _moe(h2, tok, tile_e, tile_valid, wg, wu, wd, *, tmx):
    n_tiles = tok.shape[0]
    nxt = lambda i, te, tv: (jnp.minimum(i + 1, n_tiles - 1), 0, 0)
    grid_spec = pltpu.PrefetchScalarGridSpec(
        num_scalar_prefetch=2, grid=(n_tiles,),
        in_specs=[pl.BlockSpec((1, 1, tmx), lambda i, te, tv: (i, 0, 0), memory_space=pltpu.SMEM),
                  pl.BlockSpec((1, 1, tmx), nxt, memory_space=pltpu.SMEM),
                  pl.BlockSpec(memory_space=pl.ANY),
                  pl.BlockSpec((1, D_MODEL, D_EXPERT), lambda i, te, tv: (te[i], 0, 0)),
                  pl.BlockSpec((1, D_MODEL, D_EXPERT), lambda i, te, tv: (te[i], 0, 0)),
                  pl.BlockSpec((1, D_EXPERT, D_MODEL), lambda i, te, tv: (te[i], 0, 0))],
        out_specs=pl.BlockSpec((tmx, D_MODEL), lambda i, te, tv: (i, 0)),
        scratch_shapes=[pltpu.VMEM((2, tmx, D_MODEL), F32), pltpu.SemaphoreType.DMA((2,))])
    return pl.pallas_call(
        _moe_kernel, grid_spec=grid_spec,
        out_shape=jax.ShapeDtypeStruct((n_tiles * tmx, D_MODEL), F32),
        compiler_params=_params(("arbitrary",)), name="moe",
    )(tile_e, tile_valid, tok, tok, h2, wg, wu, wd)


def _combine_kernel(p1c_ref, p2c_ref, p1n_ref, p2n_ref, x1_ref, c1_ref, c2_ref, y_hbm, o_ref, ybuf, sem):
    i = pl.program_id(0)
    n = pl.num_programs(0)
    tm = o_ref.shape[0]
    slot = i % 2

    def start(p1_ref, p2_ref, s):
        _start_row_gather(p1_ref, y_hbm, ybuf.at[s, 0], sem.at[s], tm)
        _start_row_gather(p2_ref, y_hbm, ybuf.at[s, 1], sem.at[s], tm)

    @pl.when(i == 0)
    def _():
        start(p1c_ref, p2c_ref, 0)

    @pl.when(i + 1 < n)
    def _():
        start(p1n_ref, p2n_ref, 1 - slot)

    _wait_row_gather(y_hbm, ybuf.at[slot, 0], sem.at[slot], tm)
    _wait_row_gather(y_hbm, ybuf.at[slot, 1], sem.at[slot], tm)
    reps = D_MODEL // LANES
    c1 = jnp.tile(c1_ref[...], (1, reps))
    c2 = jnp.tile(c2_ref[...], (1, reps))
    o_ref[...] = x1_ref[...] + c1 * ybuf[slot, 0] + c2 * ybuf[slot, 1]


def _combine(pos, x1, c1, c2, y, *, tm):
    T = x1.shape[0]
    n = T // tm
    p1 = pos[0].reshape(n, 1, tm)
    p2 = pos[1].reshape(n, 1, tm)
    cur = pl.BlockSpec((1, 1, tm), lambda i: (i, 0, 0), memory_space=pltpu.SMEM)
    nxt = pl.BlockSpec((1, 1, tm), lambda i: (jnp.minimum(i + 1, n - 1), 0, 0), memory_space=pltpu.SMEM)
    tok = lambda w: pl.BlockSpec((tm, w), lambda i: (i, 0))
    return pl.pallas_call(
        _combine_kernel, grid=(n,),
        in_specs=[cur, cur, nxt, nxt, tok(D_MODEL), tok(LANES), tok(LANES), pl.BlockSpec(memory_space=pl.ANY)],
        out_specs=tok(D_MODEL),
        out_shape=jax.ShapeDtypeStruct((T, D_MODEL), F32),
        scratch_shapes=[pltpu.VMEM((2, TOP_K, tm, D_MODEL), F32), pltpu.SemaphoreType.DMA((2,))],
        compiler_params=_params(("arbitrary",)), name="combine",
    )(p1, p2, p1, p2, x1, c1, c2, y)


def _tiles(S):
    return dict(tm=min(512, S), tq=min(256, S), cb=min(256, S), tmerge=min(512, S), troute=min(1024, S),
                tmoe=256, tcomb=256)


def _rope_tables(S):
    rows = S // GRID_W
    row_ids = jnp.repeat(jnp.arange(rows), GRID_W).astype(F32)
    col_ids = jnp.tile(jnp.arange(GRID_W), rows).astype(F32)
    inv_freq = ROPE_THETA ** (-jnp.arange(0, HALF, 2, dtype=F32) / HALF)
    ang = jnp.concatenate([row_ids[:, None] * inv_freq, col_ids[:, None] * inv_freq], axis=-1)
    return jnp.cos(ang), jnp.sin(ang)


def kernel(x, g_mix, w_in, q_norm, k_norm, hgrn_norm, lb_fwd, lb_bwd, w_attn_branch, w_hgrn_branch, w_out, g_ffn,
           w_router_group, b_router_group, w_router_expert, b_router_expert, w_exp_gate, w_exp_up, w_exp_down):
    B, S, D = x.shape
    assert D == D_MODEL and w_in.shape == (1, D_MODEL, D_IN) and lb_fwd.shape[0] == 2
    T = B * S
    t = _tiles(S)

    perm = np.concatenate([np.arange(0, HEAD_DIM, 2), np.arange(1, HEAD_DIM, 2)])
    qcols = (np.arange(ATTN_HEADS)[:, None] * HEAD_DIM + perm[None, :]).reshape(-1)
    kcols = IN_OFF[1] + (np.arange(ATTN_KV_HEADS)[:, None] * HEAD_DIM + perm[None, :]).reshape(-1)
    cols = np.concatenate([qcols, kcols, np.arange(IN_OFF[2], D_IN)])
    w_all = w_in[0][:, cols].astype(BF16)
    gq = q_norm[0][perm][:, None].astype(F32)
    gk = k_norm[0][perm][:, None].astype(F32)
    cos, sin = _rope_tables(S)

    (qt, k, kn2, vt, hq, hv, kf, lff, kb, lfb, og, sa, sb) = _in_proj(
        x.reshape(T, D), g_mix.astype(F32), w_all, gq, gk, cos.T, sin.T,
        lb_fwd.astype(F32), lb_bwd.astype(F32), B=B, S=S, tm=t["tm"])

    ao = _attention(qt, k, kn2, vt, B=B, S=S, tq=t["tq"]).reshape(T, ATTN_Q_W)
    of, ob = _hgrn(hq, hv, kf, lff, kb, lfb, B=B, S=S, cb=t["cb"])

    wr = jnp.zeros((D_MODEL, ROUTE_W), F32)
    wr = wr.at[:, 0:N_GROUPS].set(w_router_group[0]).at[:, EXPERT_ROW0:EXPERT_ROW0 + N_EXPERTS].set(w_router_expert[0])
    wr_hi = wr.astype(BF16)
    wr_lo = (wr - wr_hi.astype(F32)).astype(BF16)
    rbias = jnp.zeros((ROUTE_W, 1), F32)
    rbias = rbias.at[0:N_GROUPS, 0].set(b_router_group[0]).at[EXPERT_ROW0:EXPERT_ROW0 + N_EXPERTS, 0].set(
        b_router_expert[0])
    x1, h2, lg = _merge(
        ao, of, ob, og, sa, sb, x.reshape(T, D), w_attn_branch[0].astype(BF16), w_hgrn_branch[0].astype(BF16),
        w_out[0].astype(BF16), jnp.tile(hgrn_norm[0], HGRN_HEADS)[None, :].astype(F32), g_ffn.astype(F32),
        jnp.stack([wr_hi, wr_lo]), tm=t["tmerge"])
    ids, c1, c2 = _route(lg, rbias, tm=t["troute"])
    tok, pos, tile_e, tile_valid = _routing_tables(ids[0:TOP_K], t["tmoe"])
    y = _moe(h2, tok, tile_e, tile_valid, w_exp_gate[0].astype(BF16), w_exp_up[0].astype(BF16),
             w_exp_down[0].astype(BF16), tmx=t["tmoe"])
    out = _combine(pos, x1, c1, c2, y, tm=t["tcomb"])
    return out.reshape(B, S, D)
```

```python
import functools
import math

import jax
import jax.numpy as jnp
import numpy as np
from jax import lax
from jax.experimental import pallas as pl
from jax.experimental.pallas import tpu as pltpu

F32 = jnp.float32
BF16 = jnp.bfloat16

D_MODEL = 1024
GRID_W = 64
EPS = 1e-6
ATTN_HEADS = 8
ATTN_KV_HEADS = 2
GQA_GROUP = ATTN_HEADS // ATTN_KV_HEADS
HEAD_DIM = 64
HALF = HEAD_DIM // 2
ROPE_THETA = 10000.0
HGRN_HEADS = 4
HGRN_DK = 128
HGRN_CHUNK = 32
HGRN_SCALE = HGRN_DK ** -0.5
N_GROUPS = 4
EPG = 8
N_EXPERTS = N_GROUPS * EPG
TOP_K = 2
D_EXPERT = 512
ATTN_Q_W = ATTN_HEADS * HEAD_DIM
ATTN_KV_W = ATTN_KV_HEADS * HEAD_DIM
HGRN_W = HGRN_HEADS * HGRN_DK
IN_SPLITS = (ATTN_Q_W, ATTN_KV_W, ATTN_KV_W, HGRN_W, HGRN_W, HGRN_W, HGRN_W, HGRN_W, D_MODEL, D_MODEL)
IN_OFF = tuple(int(v) for v in np.cumsum((0,) + IN_SPLITS))
D_IN = IN_OFF[-1]
LANES = 128
ROUTE_W = LANES
EXPERT_ROW0 = 8
VMEM_LIMIT = 56 * 1024 * 1024
NEG_BIG = -1e30
LOG2E = math.log2(math.e)
V_ROWS = HEAD_DIM + 16
SAFE_SCORE_BOUND = 60.0


def _sigmoid(x):
    return 1.0 / (1.0 + jnp.exp(-x))


def _silu(x):
    return x * _sigmoid(x)


def _dot(a, b):
    return jnp.dot(a, b, preferred_element_type=F32)


def _params(sem):
    return pltpu.CompilerParams(dimension_semantics=sem, vmem_limit_bytes=VMEM_LIMIT)


def _const_spec(shape):
    nd = len(shape)
    return pl.BlockSpec(shape, lambda *_: (0,) * nd)


ROW_TILES = D_MODEL // LANES


def _store_token_tiles(ref, val):
    rows = val.shape[0]
    for j in range(ROW_TILES):
        ref[pl.ds(j, rows, stride=ROW_TILES), :] = val[:, LANES * j:LANES * (j + 1)]


def _load_token_tiles(ref):
    rows = ref.shape[0] // ROW_TILES
    return jnp.concatenate([ref[pl.ds(j, rows, stride=ROW_TILES), :] for j in range(ROW_TILES)], axis=1)


def _row_tile(ref, r):
    return ref.at[pl.ds(pl.multiple_of(r * ROW_TILES, ROW_TILES), ROW_TILES), :]


def _inproj_kernel(x_ref, g_ref, w_ref, gq_ref, gk_ref, cos_ref, sin_ref, lbf_ref, lbb_ref,
                   qt_ref, k_ref, kn2_ref, vt_ref, hq_ref, hv_ref, kf_ref, lff_ref, kb_ref, lfb_ref,
                   og_ref, sa_ref, sb_ref):
    tm = x_ref.shape[0]
    x = x_ref[...]
    ms = jnp.mean(x * x, axis=-1, keepdims=True)
    h = (x * lax.rsqrt(ms + EPS) * g_ref[...]).astype(BF16)

    def proj(i):
        return _dot(h, w_ref[:, IN_OFF[i]:IN_OFF[i + 1]])

    def norm_rope_t(y, n_heads, gain_ref):
        yt = y.T.reshape(n_heads, HEAD_DIM, tm)
        ss = jnp.sum(yt * yt, axis=1, keepdims=True)
        yn = yt * lax.rsqrt(ss * (1.0 / HEAD_DIM) + EPS) * gain_ref[...][None]
        a, b = yn[:, :HALF, :], yn[:, HALF:, :]
        c, s = cos_ref[...][None], sin_ref[...][None]
        return jnp.concatenate([a * c - b * s, a * s + b * c], axis=1)

    qt_ref[0] = (norm_rope_t(proj(0), ATTN_HEADS, gq_ref) * (HEAD_DIM ** -0.5 * LOG2E)).astype(BF16)
    kt = norm_rope_t(proj(1), ATTN_KV_HEADS, gk_ref).astype(BF16).astype(F32)
    kn2_ref[0] = jnp.sum(kt * kt, axis=1)
    k = kt.reshape(ATTN_KV_W, tm).T
    for gg in range(ATTN_KV_HEADS):
        k_ref[0, gg] = k[:, HEAD_DIM * gg:HEAD_DIM * (gg + 1)].astype(BF16)
    vt = proj(2).T.reshape(ATTN_KV_HEADS, HEAD_DIM, tm)
    ones = jnp.ones((ATTN_KV_HEADS, V_ROWS - HEAD_DIM, tm), F32)
    vt_ref[0, :, 0] = jnp.concatenate([vt, ones], axis=1).astype(BF16)

    hq_ref[...] = (_silu(proj(3)) * HGRN_SCALE).astype(BF16)

    def direction(z, lbraw_ref, k_out, lf_out):
        raw = lbraw_ref[...]
        e = jnp.exp(raw - jnp.max(raw, axis=0, keepdims=True))
        lb = e[0:1] / jnp.sum(e, axis=0, keepdims=True)
        k_out[...] = ((1.0 - lb) * _sigmoid(-z)).astype(BF16)
        lf_out[...] = jnp.log(lb + (1.0 - lb) * _sigmoid(z))

    direction(proj(4), lbf_ref, kf_ref, lff_ref)
    direction(proj(5), lbb_ref, kb_ref, lfb_ref)
    hv_ref[...] = proj(6).astype(BF16)
    og_ref[...] = _silu(proj(7)).astype(BF16)
    sa_ref[...] = _sigmoid(proj(8)).astype(BF16)
    sb_ref[...] = _sigmoid(proj(9)).astype(BF16)


def _in_proj(x2d, g_mix, w_all, gq, gk, cos_t, sin_t, lb_fwd, lb_bwd, *, B, S, tm):
    T = B * S
    nsb = S // tm
    tok = lambda w: pl.BlockSpec((tm, w), lambda i: (i, 0))
    in_specs = [
        tok(D_MODEL), _const_spec((1, D_MODEL)), _const_spec((D_MODEL, D_IN)),
        _const_spec((HEAD_DIM, 1)), _const_spec((HEAD_DIM, 1)),
        pl.BlockSpec((HALF, tm), lambda i: (0, i % nsb)), pl.BlockSpec((HALF, tm), lambda i: (0, i % nsb)),
        _const_spec(lb_fwd.shape), _const_spec(lb_bwd.shape),
    ]
    out_shape = [
        jax.ShapeDtypeStruct((B, ATTN_HEADS, HEAD_DIM, S), BF16),
        jax.ShapeDtypeStruct((B, ATTN_KV_HEADS, S, HEAD_DIM), BF16),
        jax.ShapeDtypeStruct((B, ATTN_KV_HEADS, S), F32),
        jax.ShapeDtypeStruct((B, ATTN_KV_HEADS, nsb, V_ROWS, tm), BF16),
        jax.ShapeDtypeStruct((T, HGRN_W), BF16),
        jax.ShapeDtypeStruct((T, HGRN_W), BF16),
        jax.ShapeDtypeStruct((T, HGRN_W), BF16),
        jax.ShapeDtypeStruct((T, HGRN_W), F32),
        jax.ShapeDtypeStruct((T, HGRN_W), BF16),
        jax.ShapeDtypeStruct((T, HGRN_W), F32),
        jax.ShapeDtypeStruct((T, HGRN_W), BF16),
        jax.ShapeDtypeStruct((T, D_MODEL), BF16),
        jax.ShapeDtypeStruct((T, D_MODEL), BF16),
    ]
    out_specs = [
        pl.BlockSpec((1, ATTN_HEADS, HEAD_DIM, tm), lambda i: (i // nsb, 0, 0, i % nsb)),
        pl.BlockSpec((1, ATTN_KV_HEADS, tm, HEAD_DIM), lambda i: (i // nsb, 0, i % nsb, 0)),
        pl.BlockSpec((1, ATTN_KV_HEADS, tm), lambda i: (i // nsb, 0, i % nsb)),
        pl.BlockSpec((1, ATTN_KV_HEADS, 1, V_ROWS, tm), lambda i: (i // nsb, 0, i % nsb, 0, 0)),
        tok(HGRN_W), tok(HGRN_W), tok(HGRN_W), tok(HGRN_W), tok(HGRN_W), tok(HGRN_W), tok(HGRN_W),
        tok(D_MODEL), tok(D_MODEL),
    ]
    return pl.pallas_call(
        _inproj_kernel, grid=(T // tm,), in_specs=in_specs, out_specs=out_specs, out_shape=out_shape,
        compiler_params=_params(("parallel",)), name="in_proj",
    )(x2d, g_mix, w_all, gq, gk, cos_t, sin_t, lb_fwd, lb_bwd)


def _attn_kernel(qt_ref, k_ref, kn2_ref, vt_ref, o_ref, shift_ref, acc_ref):
    g = pl.program_id(1)
    nk, tk = vt_ref.shape[2], vt_ref.shape[4]

    def keys(j):
        return k_ref[0, 0, pl.ds(pl.multiple_of(j * tk, tk), tk), :]

    kmax2 = jnp.max(kn2_ref[0, pl.ds(g, 1), :], axis=1, keepdims=True)
    for hh in range(GQA_GROUP):
        q = qt_ref[0, hh].astype(F32)
        shift_ref[hh] = jnp.sqrt(jnp.sum(q * q, axis=0, keepdims=True) * kmax2)

    @pl.when(jnp.max(shift_ref[...]) > SAFE_SCORE_BOUND)
    def _():
        shift_ref[...] = jnp.full(shift_ref.shape, NEG_BIG, F32)

        def max_body(j, carry):
            k = keys(j)
            for hh in range(GQA_GROUP):
                st = _dot(k, qt_ref[0, hh])
                shift_ref[hh] = jnp.maximum(shift_ref[hh], jnp.max(st, axis=0, keepdims=True))
            return carry

        lax.fori_loop(0, nk, max_body, 0)

    acc_ref[...] = jnp.zeros(acc_ref.shape, F32)

    def body(j, carry):
        k = keys(j)
        vt = vt_ref[0, 0, j]
        sts = [_dot(k, qt_ref[0, hh]) for hh in range(GQA_GROUP)]
        for hh in range(GQA_GROUP):
            pt = jnp.exp2(sts[hh] - shift_ref[hh])
            acc_ref[hh] += _dot(vt, pt.astype(BF16))
        return carry

    lax.fori_loop(0, nk, body, 0)
    outs = []
    for hh in range(GQA_GROUP):
        acc = acc_ref[hh]
        outs.append(acc[:HEAD_DIM] * (1.0 / acc[HEAD_DIM:HEAD_DIM + 1]))
    o_ref[0] = jnp.concatenate(outs, axis=0).T.astype(BF16)


def _attention(qt, k, kn2, vt, *, B, S, tq):
    nk, tk = vt.shape[2], vt.shape[4]
    gw = GQA_GROUP * HEAD_DIM
    return pl.pallas_call(
        _attn_kernel, grid=(B, ATTN_KV_HEADS, S // tq),
        in_specs=[
            pl.BlockSpec((1, GQA_GROUP, HEAD_DIM, tq), lambda b, g, i: (b, g, 0, i)),
            pl.BlockSpec((1, 1, S, HEAD_DIM), lambda b, g, i: (b, g, 0, 0)),
            pl.BlockSpec((1, ATTN_KV_HEADS, S), lambda b, g, i: (b, 0, 0)),
            pl.BlockSpec((1, 1, nk, V_ROWS, tk), lambda b, g, i: (b, g, 0, 0, 0)),
        ],
        out_specs=pl.BlockSpec((1, tq, gw), lambda b, g, i: (b, i, g)),
        out_shape=jax.ShapeDtypeStruct((B, S, ATTN_Q_W), BF16),
        scratch_shapes=[pltpu.VMEM((GQA_GROUP, 1, tq), F32), pltpu.VMEM((GQA_GROUP, V_ROWS, tq), F32)],
        compiler_params=_params(("parallel", "parallel", "parallel")), name="attn",
    )(qt, k, kn2, vt)


def _hgrn_run_matrix(cb, reverse):
    C = HGRN_CHUNK
    t = np.arange(cb)[:, None]
    s = np.arange(cb)[None, :]
    same = (t // C) == (s // C)
    return jnp.asarray(same & ((s >= t) if reverse else (s <= t)), dtype=BF16)


_TN = (((0,), (0,)), ((), ()))
_NT = (((1,), (1,)), ((), ()))


def _hgrn_kernel(qf_ref, vf_ref, kf_ref, lff_ref, qb_ref, vb_ref, kb_ref, lfb_ref, runf_ref, runb_ref,
                 of_ref, ob_ref, stf_ref, stb_ref):
    @pl.when(pl.program_id(1) == 0)
    def _():
        stf_ref[...] = jnp.zeros(stf_ref.shape, F32)
        stb_ref[...] = jnp.zeros(stb_ref.shape, F32)

    cb = qf_ref.shape[0]
    C = HGRN_CHUNK
    n = cb // C
    row = lax.broadcasted_iota(jnp.int32, (cb, cb), 0)
    col = lax.broadcasted_iota(jnp.int32, (cb, cb), 1)
    same = (row // C) == (col // C)
    dirs = [
        dict(q=qf_ref, v=vf_ref, k=kf_ref, lf=lff_ref, run=runf_ref, o=of_ref, st=stf_ref, rev=False,
             mask=same & (col <= row), last=C - 1, mid=C // 2),
        dict(q=qb_ref, v=vb_ref, k=kb_ref, lf=lfb_ref, run=runb_ref, o=ob_ref, st=stb_ref, rev=True,
             mask=same & (col >= row), last=0, mid=C - 1 - C // 2),
    ]
    heads = [slice(HGRN_DK * hh, HGRN_DK * (hh + 1)) for hh in range(HGRN_HEADS)]

    for d in dirs:
        lf = d["lf"][...]
        lf_hi = lf.astype(BF16)
        lf_lo = (lf - lf_hi.astype(F32)).astype(BF16)
        d["b"] = _dot(d["run"][...], lf_hi) + _dot(d["run"][...], lf_lo)
    for d in dirs:
        b = d["b"]
        b3 = b.reshape(n, C, HGRN_W)
        b_last = jnp.broadcast_to(b3[:, d["last"]:d["last"] + 1, :], b3.shape).reshape(cb, HGRN_W)
        b_ref = jnp.broadcast_to(b3[:, d["mid"]:d["mid"] + 1, :], b3.shape).reshape(cb, HGRN_W)
        q = d["q"][...].astype(F32)
        k = d["k"][...].astype(F32)
        d["qs"] = (q * jnp.exp(b - b_ref)).astype(BF16)
        d["ks"] = (k * jnp.exp(b_ref - b)).astype(BF16)
        d["qi"] = (q * jnp.exp(b)).astype(BF16)
        d["kst"] = (k * jnp.exp(b_last - b)).astype(BF16)
        d["dec"] = jnp.exp(b_last)
        d["vv"] = d["v"][...]
    for d in dirs:
        d["a"] = [jnp.where(d["mask"], lax.dot_general(d["qs"][:, sl], d["ks"][:, sl], _NT,
                                                       preferred_element_type=F32), 0.0).astype(BF16)
                  for sl in heads]
    for d in dirs:
        d["oi"] = [_dot(d["a"][hh], d["vv"][:, sl]) for hh, sl in enumerate(heads)]
    for d in dirs:
        d["upd"] = [[lax.dot_general(d["vv"][C * ci:C * (ci + 1), sl], d["kst"][C * ci:C * (ci + 1), sl], _TN,
                                     preferred_element_type=F32) for ci in range(n)] for sl in heads]
    for d in dirs:
        order = range(n - 1, -1, -1) if d["rev"] else range(n)
        d["states"] = []
        for hh, sl in enumerate(heads):
            st = d["st"][hh]
            seen = [None] * n
            for ci in order:
                seen[ci] = st.astype(BF16)
                st = d["dec"][C * ci:C * ci + 1, sl] * st + d["upd"][hh][ci]
            d["st"][hh] = st
            d["states"].append(seen)
    for d in dirs:
        for hh, sl in enumerate(heads):
            inter = [lax.dot_general(d["qi"][C * ci:C * (ci + 1), sl], d["states"][hh][ci], _NT,
                                     preferred_element_type=F32) for ci in range(n)]
            d["o"][:, sl] = d["oi"][hh] + jnp.concatenate(inter, axis=0)


def _hgrn(hq, hv, kf, lff, kb, lfb, *, B, S, cb):
    T = B * S
    nblk = S // cb
    fwd = pl.BlockSpec((cb, HGRN_W), lambda b, i: (b * nblk + i, 0))
    bwd = pl.BlockSpec((cb, HGRN_W), lambda b, i: (b * nblk + nblk - 1 - i, 0))
    run = _const_spec((cb, cb))
    st = pltpu.VMEM((HGRN_HEADS, HGRN_DK, HGRN_DK), F32)
    return pl.pallas_call(
        _hgrn_kernel, grid=(B, nblk),
        in_specs=[fwd, fwd, fwd, fwd, bwd, bwd, bwd, bwd, run, run],
        out_specs=[fwd, bwd],
        out_shape=[jax.ShapeDtypeStruct((T, HGRN_W), F32)] * 2,
        scratch_shapes=[st, st],
        compiler_params=_params(("parallel", "arbitrary")), name="hgrn",
    )(hq, hv, kf, lff, hq, hv, kb, lfb, _hgrn_run_matrix(cb, False), _hgrn_run_matrix(cb, True))


def _merge_kernel(ao_ref, of_ref, ob_ref, og_ref, sa_ref, sb_ref, x_ref, wa_ref, wb_ref, wo_ref,
                  gh_ref, gf_ref, wr_ref, x1_ref, h2_ref, lg_ref):
    o = of_ref[...] + ob_ref[...]
    parts = []
    for hh in range(HGRN_HEADS):
        oh = o[:, HGRN_DK * hh:HGRN_DK * (hh + 1)]
        parts.append(oh * lax.rsqrt(jnp.mean(oh * oh, axis=1, keepdims=True) + EPS))
    hn = jnp.concatenate(parts, axis=1) * gh_ref[...] * og_ref[...].astype(F32)
    merged = (sa_ref[...].astype(F32) * _dot(ao_ref[...], wa_ref[...])
              + sb_ref[...].astype(F32) * _dot(hn.astype(BF16), wb_ref[...]))
    x1 = x_ref[...] + _dot(merged.astype(BF16), wo_ref[...])
    x1_ref[...] = x1
    h2 = x1 * lax.rsqrt(jnp.mean(x1 * x1, axis=1, keepdims=True) + EPS) * gf_ref[...]
    hi = h2.astype(BF16)
    lo = (h2 - hi.astype(F32)).astype(BF16)
    _store_token_tiles(h2_ref, h2)
    lg_ref[...] = _dot(hi, wr_ref[0]) + _dot(lo, wr_ref[0]) + _dot(hi, wr_ref[1])


def _merge(ao, of, ob, og, sa, sb, x2d, wa, wb, wo, gh, gf, wr, *, tm):
    T = x2d.shape[0]
    tok = lambda w: pl.BlockSpec((tm, w), lambda i: (i, 0))
    return pl.pallas_call(
        _merge_kernel, grid=(T // tm,),
        in_specs=[tok(ATTN_Q_W), tok(HGRN_W), tok(HGRN_W), tok(HGRN_W), tok(D_MODEL), tok(D_MODEL), tok(D_MODEL),
                  _const_spec(wa.shape), _const_spec(wb.shape), _const_spec(wo.shape),
                  _const_spec(gh.shape), _const_spec(gf.shape), _const_spec(wr.shape)],
        out_specs=[tok(D_MODEL), pl.BlockSpec((tm * ROW_TILES, LANES), lambda i: (i, 0)), tok(ROUTE_W)],
        out_shape=[jax.ShapeDtypeStruct((T, D_MODEL), F32), jax.ShapeDtypeStruct((T * ROW_TILES, LANES), F32),
                   jax.ShapeDtypeStruct((T, ROUTE_W), F32)],
        compiler_params=_params(("parallel",)), name="merge",
    )(ao, of, ob, og, sa, sb, x2d, wa, wb, wo, gh, gf, wr)


def _first_argmax(vals, n):
    top = jnp.max(vals, axis=0, keepdims=True)
    rows = lax.broadcasted_iota(jnp.int32, vals.shape, 0)
    idx = jnp.min(jnp.where(vals == top, rows, n), axis=0, keepdims=True)
    return top, idx, rows


def _route_kernel(lg_ref, bias_ref, ids_ref, c1_ref, c2_ref):
    lt = lg_ref[...].T + bias_ref[...]
    gl = lt[0:N_GROUPS]
    ge = jnp.exp(gl - jnp.max(gl, axis=0, keepdims=True))
    gp = ge / jnp.sum(ge, axis=0, keepdims=True)
    p_g, g_idx, _ = _first_argmax(gp, N_GROUPS)
    esel = lt[EXPERT_ROW0:EXPERT_ROW0 + EPG]
    for gg in range(1, N_GROUPS):
        esel = jnp.where(g_idx == gg, lt[EXPERT_ROW0 + EPG * gg:EXPERT_ROW0 + EPG * (gg + 1)], esel)
    ee = jnp.exp(esel - jnp.max(esel, axis=0, keepdims=True))
    pe = ee / jnp.sum(ee, axis=0, keepdims=True)
    p1, i1, rows = _first_argmax(pe, EPG)
    p2, i2, _ = _first_argmax(jnp.where(rows == i1, -1.0, pe), EPG)
    den = p1 + p2
    c1 = p_g * (p1 / den)
    c2 = p_g * (p2 / den)
    e1 = g_idx * EPG + i1
    e2 = g_idx * EPG + i2
    tm = lt.shape[1]
    ids_ref[...] = jnp.concatenate([e1, e2, jnp.zeros((6, tm), jnp.int32)], axis=0)
    c1_ref[...] = jnp.broadcast_to(c1, (LANES, tm)).T
    c2_ref[...] = jnp.broadcast_to(c2, (LANES, tm)).T


def _route(lg, bias, *, tm):
    T = lg.shape[0]
    tok = pl.BlockSpec((tm, LANES), lambda i: (i, 0))
    return pl.pallas_call(
        _route_kernel, grid=(T // tm,),
        in_specs=[pl.BlockSpec((tm, ROUTE_W), lambda i: (i, 0)), _const_spec((ROUTE_W, 1))],
        out_specs=[pl.BlockSpec((8, tm), lambda i: (0, i)), tok, tok],
        out_shape=[jax.ShapeDtypeStruct((8, T), jnp.int32), jax.ShapeDtypeStruct((T, LANES), F32),
                   jax.ShapeDtypeStruct((T, LANES), F32)],
        compiler_params=_params(("parallel",)), name="route",
    )(lg, bias)


def _routing_tables(ids, tmx):
    T = ids.shape[1]
    P = TOP_K * T
    ef = ids.reshape(P)
    onehot = (ef[:, None] == jnp.arange(N_EXPERTS, dtype=jnp.int32)[None, :]).astype(jnp.int32)
    csum = jnp.cumsum(onehot, axis=0)
    cnt = csum[-1]
    rank = jnp.take_along_axis(csum, ef[:, None], axis=1)[:, 0] - 1
    padded = ((cnt + tmx - 1) // tmx) * tmx
    off = jnp.cumsum(padded) - padded
    pos = (off[ef] + rank).astype(jnp.int32)
    n_rows = P + N_EXPERTS * tmx
    n_tiles = n_rows // tmx
    tok = jnp.zeros((n_rows,), jnp.int32).at[pos].set(jnp.arange(P, dtype=jnp.int32) % T, unique_indices=True)
    ends = off + padded
    tile_start = jnp.arange(n_tiles, dtype=jnp.int32) * tmx
    tile_e = jnp.sum((tile_start[:, None] >= ends[None, :]).astype(jnp.int32), axis=1)
    tile_valid = (tile_e < N_EXPERTS).astype(jnp.int32)
    tile_e = jnp.minimum(tile_e, N_EXPERTS - 1)
    return tok.reshape(n_tiles, 1, tmx), pos.reshape(TOP_K, T), tile_e, tile_valid


def _start_row_gather(idx_ref, src_hbm, dst, sem, n_rows):
    def issue(r, carry):
        pltpu.make_async_copy(_row_tile(src_hbm, idx_ref[0, 0, r]), _row_tile(dst, r), sem).start()
        return carry

    lax.fori_loop(0, n_rows, issue, 0, unroll=8)


def _wait_row_gather(src_hbm, dst, sem, n_rows):
    pltpu.make_async_copy(src_hbm.at[pl.ds(0, n_rows * ROW_TILES), :], dst, sem).wait()


def _moe_kernel(te_ref, tv_ref, tok_cur_ref, tok_nxt_ref, h_hbm, wg_ref, wu_ref, wd_ref, y_ref, xbuf, sem):
    i = pl.program_id(0)
    n = pl.num_programs(0)
    tmx = xbuf.shape[1] // ROW_TILES
    slot = i % 2

    @pl.when((i == 0) & (tv_ref[0] == 1))
    def _():
        _start_row_gather(tok_cur_ref, h_hbm, xbuf.at[0], sem.at[0], tmx)

    @pl.when((i + 1 < n) & (tv_ref[jnp.minimum(i + 1, n - 1)] == 1))
    def _():
        _start_row_gather(tok_nxt_ref, h_hbm, xbuf.at[1 - slot], sem.at[1 - slot], tmx)

    @pl.when(tv_ref[i] == 1)
    def _():
        _wait_row_gather(h_hbm, xbuf.at[slot], sem.at[slot], tmx)
        x = _load_token_tiles(xbuf.at[slot]).astype(BF16)
        hid = _silu(_dot(x, wg_ref[0])) * _dot(x, wu_ref[0])
        _store_token_tiles(y_ref, _dot(hid.astype(BF16), wd_ref[0]))

    @pl.when(tv_ref[i] == 0)
    def _():
        y_ref[...] = jnp.zeros(y_ref.shape, F32)


def _moe(h2, tok, tile_e, tile_valid, wg, wu, wd, *, tmx):
    n_tiles = tok.shape[0]
    nxt = lambda i, te, tv: (jnp.minimum(i + 1, n_tiles - 1), 0, 0)
    grid_spec = pltpu.PrefetchScalarGridSpec(
        num_scalar_prefetch=2, grid=(n_tiles,),
        in_specs=[pl.BlockSpec((1, 1, tmx), lambda i, te, tv: (i, 0, 0), memory_space=pltpu.SMEM),
                  pl.BlockSpec((1, 1, tmx), nxt, memory_space=pltpu.SMEM),
                  pl.BlockSpec(memory_space=pl.ANY),
                  pl.BlockSpec((1, D_MODEL, D_EXPERT), lambda i, te, tv: (te[i], 0, 0)),
                  pl.BlockSpec((1, D_MODEL, D_EXPERT), lambda i, te, tv: (te[i], 0, 0)),
                  pl.BlockSpec((1, D_EXPERT, D_MODEL), lambda i, te, tv: (te[i], 0, 0))],
        out_specs=pl.BlockSpec((tmx * ROW_TILES, LANES), lambda i, te, tv: (i, 0)),
        scratch_shapes=[pltpu.VMEM((2, tmx * ROW_TILES, LANES), F32), pltpu.SemaphoreType.DMA((2,))])
    return pl.pallas_call(
        _moe_kernel, grid_spec=grid_spec,
        out_shape=jax.ShapeDtypeStruct((n_tiles * tmx * ROW_TILES, LANES), F32),
        compiler_params=_params(("arbitrary",)), name="moe",
    )(tile_e, tile_valid, tok, tok, h2, wg, wu, wd)


def _combine_kernel(p1c_ref, p2c_ref, p1n_ref, p2n_ref, x1_ref, c1_ref, c2_ref, y_hbm, o_ref, ybuf, sem):
    i = pl.program_id(0)
    n = pl.num_programs(0)
    tm = o_ref.shape[0]
    slot = i % 2

    def start(p1_ref, p2_ref, s):
        _start_row_gather(p1_ref, y_hbm, ybuf.at[s, 0], sem.at[s], tm)
        _start_row_gather(p2_ref, y_hbm, ybuf.at[s, 1], sem.at[s], tm)

    @pl.when(i == 0)
    def _():
        start(p1c_ref, p2c_ref, 0)

    @pl.when(i + 1 < n)
    def _():
        start(p1n_ref, p2n_ref, 1 - slot)

    _wait_row_gather(y_hbm, ybuf.at[slot, 0], sem.at[slot], tm)
    _wait_row_gather(y_hbm, ybuf.at[slot, 1], sem.at[slot], tm)
    reps = D_MODEL // LANES
    c1 = jnp.tile(c1_ref[...], (1, reps))
    c2 = jnp.tile(c2_ref[...], (1, reps))
    o_ref[...] = (x1_ref[...] + c1 * _load_token_tiles(ybuf.at[slot, 0])
                  + c2 * _load_token_tiles(ybuf.at[slot, 1]))


def _combine(pos, x1, c1, c2, y, *, tm):
    T = x1.shape[0]
    n = T // tm
    p1 = pos[0].reshape(n, 1, tm)
    p2 = pos[1].reshape(n, 1, tm)
    cur = pl.BlockSpec((1, 1, tm), lambda i: (i, 0, 0), memory_space=pltpu.SMEM)
    nxt = pl.BlockSpec((1, 1, tm), lambda i: (jnp.minimum(i + 1, n - 1), 0, 0), memory_space=pltpu.SMEM)
    tok = lambda w: pl.BlockSpec((tm, w), lambda i: (i, 0))
    return pl.pallas_call(
        _combine_kernel, grid=(n,),
        in_specs=[cur, cur, nxt, nxt, tok(D_MODEL), tok(LANES), tok(LANES), pl.BlockSpec(memory_space=pl.ANY)],
        out_specs=tok(D_MODEL),
        out_shape=jax.ShapeDtypeStruct((T, D_MODEL), F32),
        scratch_shapes=[pltpu.VMEM((2, TOP_K, tm * ROW_TILES, LANES), F32), pltpu.SemaphoreType.DMA((2,))],
        compiler_params=_params(("arbitrary",)), name="combine",
    )(p1, p2, p1, p2, x1, c1, c2, y)


def _tiles(S):
    return dict(tm=min(512, S), tq=min(256, S), cb=min(256, S), tmerge=min(512, S), troute=min(1024, S),
                tmoe=256, tcomb=256)


def _rope_tables(S):
    rows = S // GRID_W
    row_ids = jnp.repeat(jnp.arange(rows), GRID_W).astype(F32)
    col_ids = jnp.tile(jnp.arange(GRID_W), rows).astype(F32)
    inv_freq = ROPE_THETA ** (-jnp.arange(0, HALF, 2, dtype=F32) / HALF)
    ang = jnp.concatenate([row_ids[:, None] * inv_freq, col_ids[:, None] * inv_freq], axis=-1)
    return jnp.cos(ang), jnp.sin(ang)


def kernel(x, g_mix, w_in, q_norm, k_norm, hgrn_norm, lb_fwd, lb_bwd, w_attn_branch, w_hgrn_branch, w_out, g_ffn,
           w_router_group, b_router_group, w_router_expert, b_router_expert, w_exp_gate, w_exp_up, w_exp_down):
    B, S, D = x.shape
    assert D == D_MODEL and w_in.shape == (1, D_MODEL, D_IN) and lb_fwd.shape[0] == 2
    T = B * S
    t = _tiles(S)

    perm = np.concatenate([np.arange(0, HEAD_DIM, 2), np.arange(1, HEAD_DIM, 2)])
    qcols = (np.arange(ATTN_HEADS)[:, None] * HEAD_DIM + perm[None, :]).reshape(-1)
    kcols = IN_OFF[1] + (np.arange(ATTN_KV_HEADS)[:, None] * HEAD_DIM + perm[None, :]).reshape(-1)
    cols = np.concatenate([qcols, kcols, np.arange(IN_OFF[2], D_IN)])
    w_all = w_in[0][:, cols].astype(BF16)
    gq = q_norm[0][perm][:, None].astype(F32)
    gk = k_norm[0][perm][:, None].astype(F32)
    cos, sin = _rope_tables(S)

    (qt, k, kn2, vt, hq, hv, kf, lff, kb, lfb, og, sa, sb) = _in_proj(
        x.reshape(T, D), g_mix.astype(F32), w_all, gq, gk, cos.T, sin.T,
        lb_fwd.astype(F32), lb_bwd.astype(F32), B=B, S=S, tm=t["tm"])

    ao = _attention(qt, k, kn2, vt, B=B, S=S, tq=t["tq"]).reshape(T, ATTN_Q_W)
    of, ob = _hgrn(hq, hv, kf, lff, kb, lfb, B=B, S=S, cb=t["cb"])

    wr = jnp.zeros((D_MODEL, ROUTE_W), F32)
    wr = wr.at[:, 0:N_GROUPS].set(w_router_group[0]).at[:, EXPERT_ROW0:EXPERT_ROW0 + N_EXPERTS].set(w_router_expert[0])
    wr_hi = wr.astype(BF16)
    wr_lo = (wr - wr_hi.astype(F32)).astype(BF16)
    rbias = jnp.zeros((ROUTE_W, 1), F32)
    rbias = rbias.at[0:N_GROUPS, 0].set(b_router_group[0]).at[EXPERT_ROW0:EXPERT_ROW0 + N_EXPERTS, 0].set(
        b_router_expert[0])
    x1, h2, lg = _merge(
        ao, of, ob, og, sa, sb, x.reshape(T, D), w_attn_branch[0].astype(BF16), w_hgrn_branch[0].astype(BF16),
        w_out[0].astype(BF16), jnp.tile(hgrn_norm[0], HGRN_HEADS)[None, :].astype(F32), g_ffn.astype(F32),
        jnp.stack([wr_hi, wr_lo]), tm=t["tmerge"])
    ids, c1, c2 = _route(lg, rbias, tm=t["troute"])
    tok, pos, tile_e, tile_valid = _routing_tables(ids[0:TOP_K], t["tmoe"])
    y = _moe(h2, tok, tile_e, tile_valid, w_exp_gate[0].astype(BF16), w_exp_up[0].astype(BF16),
             w_exp_down[0].astype(BF16), tmx=t["tmoe"])
    out = _combine(pos, x1, c1, c2, y, tm=t["tcomb"])
    return out.reshape(B, S, D)
```

```python
import functools
import math

import jax
import jax.numpy as jnp
import numpy as np
from jax import lax
from jax.experimental import pallas as pl
from jax.experimental.pallas import tpu as pltpu

F32 = jnp.float32
BF16 = jnp.bfloat16

D_MODEL = 1024
GRID_W = 64
EPS = 1e-6
ATTN_HEADS = 8
ATTN_KV_HEADS = 2
GQA_GROUP = ATTN_HEADS // ATTN_KV_HEADS
HEAD_DIM = 64
HALF = HEAD_DIM // 2
ROPE_THETA = 10000.0
HGRN_HEADS = 4
HGRN_DK = 128
HGRN_CHUNK = 32
HGRN_SCALE = HGRN_DK ** -0.5
N_GROUPS = 4
EPG = 8
N_EXPERTS = N_GROUPS * EPG
TOP_K = 2
D_EXPERT = 512
ATTN_Q_W = ATTN_HEADS * HEAD_DIM
ATTN_KV_W = ATTN_KV_HEADS * HEAD_DIM
HGRN_W = HGRN_HEADS * HGRN_DK
IN_SPLITS = (ATTN_Q_W, ATTN_KV_W, ATTN_KV_W, HGRN_W, HGRN_W, HGRN_W, HGRN_W, HGRN_W, D_MODEL, D_MODEL)
IN_OFF = tuple(int(v) for v in np.cumsum((0,) + IN_SPLITS))
D_IN = IN_OFF[-1]
LANES = 128
ROUTE_W = LANES
EXPERT_ROW0 = 8
VMEM_LIMIT = 56 * 1024 * 1024
NEG_BIG = -1e30
LOG2E = math.log2(math.e)
SAFE_SCORE_BOUND = 60.0


def _sigmoid(x):
    return 1.0 / (1.0 + jnp.exp(-x))


def _silu(x):
    return x * _sigmoid(x)


def _dot(a, b):
    return jnp.dot(a, b, preferred_element_type=F32)


def _params(sem):
    return pltpu.CompilerParams(dimension_semantics=sem, vmem_limit_bytes=VMEM_LIMIT)


def _const_spec(shape):
    nd = len(shape)
    return pl.BlockSpec(shape, lambda *_: (0,) * nd)


ROW_TILES = D_MODEL // LANES


def _store_token_tiles(ref, val):
    rows = val.shape[0]
    for j in range(ROW_TILES):
        ref[pl.ds(j, rows, stride=ROW_TILES), :] = val[:, LANES * j:LANES * (j + 1)]


def _load_token_tiles(ref):
    rows = ref.shape[0] // ROW_TILES
    return jnp.concatenate([ref[pl.ds(j, rows, stride=ROW_TILES), :] for j in range(ROW_TILES)], axis=1)


def _row_tile(ref, r):
    return ref.at[pl.ds(pl.multiple_of(r * ROW_TILES, ROW_TILES), ROW_TILES), :]


INPROJ_SUBTILES = 2

def _inproj_kernel(x_ref, g_ref, w_ref, gq_ref, gk_ref, cos_ref, sin_ref, lbf_ref, lbb_ref,
                   qt_ref, k_ref, kn2_ref, vt_ref, hq_ref, hv_ref, kf_ref, lff_ref, kb_ref, lfb_ref,
                   og_ref, sa_ref, sb_ref):
    sub = x_ref.shape[0] // INPROJ_SUBTILES
    rows = [slice(sub * s, sub * (s + 1)) for s in range(INPROJ_SUBTILES)]
    hs = []
    for r in rows:
        x = x_ref[r, :]
        ms = jnp.mean(x * x, axis=-1, keepdims=True)
        hs.append((x * lax.rsqrt(ms + EPS) * g_ref[...]).astype(BF16))

    def proj(i, s):
        return _dot(hs[s], w_ref[:, IN_OFF[i]:IN_OFF[i + 1]])

    def norm_rope_t(y, n_heads, gain_ref, r):
        yt = y.T.reshape(n_heads, HEAD_DIM, sub)
        ss = jnp.sum(yt * yt, axis=1, keepdims=True)
        yn = yt * lax.rsqrt(ss * (1.0 / HEAD_DIM) + EPS) * gain_ref[...][None]
        a, b = yn[:, :HALF, :], yn[:, HALF:, :]
        c, s = cos_ref[:, r][None], sin_ref[:, r][None]
        return jnp.concatenate([a * c - b * s, a * s + b * c], axis=1)

    def lower_bound(lbraw_ref):
        raw = lbraw_ref[...]
        e = jnp.exp(raw - jnp.max(raw, axis=0, keepdims=True))
        return e[0:1] / jnp.sum(e, axis=0, keepdims=True)

    def direction(z, lb, k_out, lf_out, r):
        k_out[r, :] = ((1.0 - lb) * _sigmoid(-z)).astype(BF16)
        lf_out[r, :] = jnp.log(lb + (1.0 - lb) * _sigmoid(z))

    for s, r in enumerate(rows):
        qt_ref[0, :, :, r] = (norm_rope_t(proj(0, s), ATTN_HEADS, gq_ref, r)
                              * (HEAD_DIM ** -0.5 * LOG2E)).astype(BF16)
    for s, r in enumerate(rows):
        kt = norm_rope_t(proj(1, s), ATTN_KV_HEADS, gk_ref, r).astype(BF16).astype(F32)
        kn2_ref[0, :, r] = jnp.sum(kt * kt, axis=1)
        k = kt.reshape(ATTN_KV_W, sub).T
        for gg in range(ATTN_KV_HEADS):
            k_ref[0, gg, r, :] = k[:, HEAD_DIM * gg:HEAD_DIM * (gg + 1)].astype(BF16)
    for s, r in enumerate(rows):
        vt_ref[0, :, 0, :, r] = proj(2, s).T.reshape(ATTN_KV_HEADS, HEAD_DIM, sub).astype(BF16)
    for s, r in enumerate(rows):
        hq_ref[r, :] = (_silu(proj(3, s)) * HGRN_SCALE).astype(BF16)
    lb_f, lb_b = lower_bound(lbf_ref), lower_bound(lbb_ref)
    for s, r in enumerate(rows):
        direction(proj(4, s), lb_f, kf_ref, lff_ref, r)
    for s, r in enumerate(rows):
        direction(proj(5, s), lb_b, kb_ref, lfb_ref, r)
    for s, r in enumerate(rows):
        hv_ref[r, :] = proj(6, s).astype(BF16)
    for s, r in enumerate(rows):
        og_ref[r, :] = _silu(proj(7, s)).astype(BF16)
    for s, r in enumerate(rows):
        sa_ref[r, :] = _sigmoid(proj(8, s)).astype(BF16)
    for s, r in enumerate(rows):
        sb_ref[r, :] = _sigmoid(proj(9, s)).astype(BF16)


def _in_proj(x2d, g_mix, w_all, gq, gk, cos_t, sin_t, lb_fwd, lb_bwd, *, B, S, tm):
    T = B * S
    nsb = S // tm
    tok = lambda w: pl.BlockSpec((tm, w), lambda i: (i, 0))
    in_specs = [
        tok(D_MODEL), _const_spec((1, D_MODEL)), _const_spec((D_MODEL, D_IN)),
        _const_spec((HEAD_DIM, 1)), _const_spec((HEAD_DIM, 1)),
        pl.BlockSpec((HALF, tm), lambda i: (0, i % nsb)), pl.BlockSpec((HALF, tm), lambda i: (0, i % nsb)),
        _const_spec(lb_fwd.shape), _const_spec(lb_bwd.shape),
    ]
    out_shape = [
        jax.ShapeDtypeStruct((B, ATTN_HEADS, HEAD_DIM, S), BF16),
        jax.ShapeDtypeStruct((B, ATTN_KV_HEADS, S, HEAD_DIM), BF16),
        jax.ShapeDtypeStruct((B, ATTN_KV_HEADS, S), F32),
        jax.ShapeDtypeStruct((B, ATTN_KV_HEADS, nsb, HEAD_DIM, tm), BF16),
        jax.ShapeDtypeStruct((T, HGRN_W), BF16),
        jax.ShapeDtypeStruct((T, HGRN_W), BF16),
        jax.ShapeDtypeStruct((T, HGRN_W), BF16),
        jax.ShapeDtypeStruct((T, HGRN_W), F32),
        jax.ShapeDtypeStruct((T, HGRN_W), BF16),
        jax.ShapeDtypeStruct((T, HGRN_W), F32),
        jax.ShapeDtypeStruct((T, HGRN_W), BF16),
        jax.ShapeDtypeStruct((T, D_MODEL), BF16),
        jax.ShapeDtypeStruct((T, D_MODEL), BF16),
    ]
    out_specs = [
        pl.BlockSpec((1, ATTN_HEADS, HEAD_DIM, tm), lambda i: (i // nsb, 0, 0, i % nsb)),
        pl.BlockSpec((1, ATTN_KV_HEADS, tm, HEAD_DIM), lambda i: (i // nsb, 0, i % nsb, 0)),
        pl.BlockSpec((1, ATTN_KV_HEADS, tm), lambda i: (i // nsb, 0, i % nsb)),
        pl.BlockSpec((1, ATTN_KV_HEADS, 1, HEAD_DIM, tm), lambda i: (i // nsb, 0, i % nsb, 0, 0)),
        tok(HGRN_W), tok(HGRN_W), tok(HGRN_W), tok(HGRN_W), tok(HGRN_W), tok(HGRN_W), tok(HGRN_W),
        tok(D_MODEL), tok(D_MODEL),
    ]
    return pl.pallas_call(
        _inproj_kernel, grid=(T // tm,), in_specs=in_specs, out_specs=out_specs, out_shape=out_shape,
        compiler_params=_params(("parallel",)), name="in_proj",
    )(x2d, g_mix, w_all, gq, gk, cos_t, sin_t, lb_fwd, lb_bwd)


def _attn_kernel(qt_ref, k_ref, kn2_ref, vt_ref, o_ref, shift_ref, l_ref, acc_ref):
    g = pl.program_id(1)
    nk, tk = vt_ref.shape[2], vt_ref.shape[4]

    def keys(j):
        return k_ref[0, 0, pl.ds(pl.multiple_of(j * tk, tk), tk), :]

    kmax2 = jnp.max(kn2_ref[0, pl.ds(g, 1), :], axis=1, keepdims=True)
    for hh in range(GQA_GROUP):
        q = qt_ref[0, hh].astype(F32)
        shift_ref[hh] = jnp.sqrt(jnp.sum(q * q, axis=0, keepdims=True) * kmax2)

    @pl.when(jnp.max(shift_ref[...]) > SAFE_SCORE_BOUND)
    def _():
        shift_ref[...] = jnp.full(shift_ref.shape, NEG_BIG, F32)

        def max_body(j, carry):
            k = keys(j)
            for hh in range(GQA_GROUP):
                st = _dot(k, qt_ref[0, hh])
                shift_ref[hh] = jnp.maximum(shift_ref[hh], jnp.max(st, axis=0, keepdims=True))
            return carry

        lax.fori_loop(0, nk, max_body, 0)

    acc_ref[...] = jnp.zeros(acc_ref.shape, F32)
    l_ref[...] = jnp.zeros(l_ref.shape, F32)
    tq = qt_ref.shape[3]

    def body(j, carry):
        k = keys(j)
        vt = vt_ref[0, 0, j]
        sts = [_dot(k, qt_ref[0, hh]) for hh in range(GQA_GROUP)]
        for hh in range(GQA_GROUP):
            pt = jnp.exp2(sts[hh] - shift_ref[hh])
            l_ref[hh] += jnp.sum(pt.reshape(tk // 8, 8, tq), axis=0)
            acc_ref[hh] += _dot(vt, pt.astype(BF16))
        return carry

    lax.fori_loop(0, nk, body, 0, unroll=8)
    outs = [acc_ref[hh] * (1.0 / jnp.sum(l_ref[hh], axis=0, keepdims=True)) for hh in range(GQA_GROUP)]
    o_ref[0] = jnp.concatenate(outs, axis=0).T.astype(BF16)


def _attention(qt, k, kn2, vt, *, B, S, tq):
    nk, tk = vt.shape[2], vt.shape[4]
    gw = GQA_GROUP * HEAD_DIM
    return pl.pallas_call(
        _attn_kernel, grid=(B, ATTN_KV_HEADS, S // tq),
        in_specs=[
            pl.BlockSpec((1, GQA_GROUP, HEAD_DIM, tq), lambda b, g, i: (b, g, 0, i)),
            pl.BlockSpec((1, 1, S, HEAD_DIM), lambda b, g, i: (b, g, 0, 0)),
            pl.BlockSpec((1, ATTN_KV_HEADS, S), lambda b, g, i: (b, 0, 0)),
            pl.BlockSpec((1, 1, nk, HEAD_DIM, tk), lambda b, g, i: (b, g, 0, 0, 0)),
        ],
        out_specs=pl.BlockSpec((1, tq, gw), lambda b, g, i: (b, i, g)),
        out_shape=jax.ShapeDtypeStruct((B, S, ATTN_Q_W), BF16),
        scratch_shapes=[pltpu.VMEM((GQA_GROUP, 1, tq), F32), pltpu.VMEM((GQA_GROUP, 8, tq), F32),
                        pltpu.VMEM((GQA_GROUP, HEAD_DIM, tq), F32)],
        compiler_params=_params(("parallel", "parallel", "parallel")), name="attn",
    )(qt, k, kn2, vt)


def _hgrn_run_matrix(cb, reverse):
    C = HGRN_CHUNK
    t = np.arange(cb)[:, None]
    s = np.arange(cb)[None, :]
    same = (t // C) == (s // C)
    return jnp.asarray(same & ((s >= t) if reverse else (s <= t)), dtype=BF16)


_TN = (((0,), (0,)), ((), ()))
_NT = (((1,), (1,)), ((), ()))


def _hgrn_kernel(qf_ref, vf_ref, kf_ref, lff_ref, qb_ref, vb_ref, kb_ref, lfb_ref, runf_ref, runb_ref,
                 of_ref, ob_ref, stf_ref, stb_ref):
    @pl.when(pl.program_id(1) == 0)
    def _():
        stf_ref[...] = jnp.zeros(stf_ref.shape, F32)
        stb_ref[...] = jnp.zeros(stb_ref.shape, F32)

    cb = qf_ref.shape[0]
    C = HGRN_CHUNK
    n = cb // C
    row = lax.broadcasted_iota(jnp.int32, (cb, cb), 0)
    col = lax.broadcasted_iota(jnp.int32, (cb, cb), 1)
    same = (row // C) == (col // C)
    dirs = [
        dict(q=qf_ref, v=vf_ref, k=kf_ref, lf=lff_ref, run=runf_ref, o=of_ref, st=stf_ref, rev=False,
             mask=same & (col <= row), last=C - 1, mid=C // 2),
        dict(q=qb_ref, v=vb_ref, k=kb_ref, lf=lfb_ref, run=runb_ref, o=ob_ref, st=stb_ref, rev=True,
             mask=same & (col >= row), last=0, mid=C - 1 - C // 2),
    ]
    heads = [slice(HGRN_DK * hh, HGRN_DK * (hh + 1)) for hh in range(HGRN_HEADS)]

    for d in dirs:
        lf = d["lf"][...]
        lf_hi = lf.astype(BF16)
        lf_lo = (lf - lf_hi.astype(F32)).astype(BF16)
        d["b"] = _dot(d["run"][...], lf_hi) + _dot(d["run"][...], lf_lo)
    for d in dirs:
        b = d["b"]
        b3 = b.reshape(n, C, HGRN_W)
        b_last = jnp.broadcast_to(b3[:, d["last"]:d["last"] + 1, :], b3.shape).reshape(cb, HGRN_W)
        b_ref = jnp.broadcast_to(b3[:, d["mid"]:d["mid"] + 1, :], b3.shape).reshape(cb, HGRN_W)
        q = d["q"][...].astype(F32)
        k = d["k"][...].astype(F32)
        d["qs"] = (q * jnp.exp(b - b_ref)).astype(BF16)
        d["ks"] = (k * jnp.exp(b_ref - b)).astype(BF16)
        d["qi"] = (q * jnp.exp(b)).astype(BF16)
        d["kst"] = (k * jnp.exp(b_last - b)).astype(BF16)
        d["dec"] = jnp.exp(b_last)
        d["vv"] = d["v"][...]
    for d in dirs:
        d["a"] = [jnp.where(d["mask"], lax.dot_general(d["qs"][:, sl], d["ks"][:, sl], _NT,
                                                       preferred_element_type=F32), 0.0).astype(BF16)
                  for sl in heads]
    for d in dirs:
        d["oi"] = [_dot(d["a"][hh], d["vv"][:, sl]) for hh, sl in enumerate(heads)]
    for d in dirs:
        d["upd"] = [[lax.dot_general(d["vv"][C * ci:C * (ci + 1), sl], d["kst"][C * ci:C * (ci + 1), sl], _TN,
                                     preferred_element_type=F32) for ci in range(n)] for sl in heads]
    for d in dirs:
        order = range(n - 1, -1, -1) if d["rev"] else range(n)
        d["states"] = []
        for hh, sl in enumerate(heads):
            st = d["st"][hh]
            seen = [None] * n
            for ci in order:
                seen[ci] = st.astype(BF16)
                st = d["dec"][C * ci:C * ci + 1, sl] * st + d["upd"][hh][ci]
            d["st"][hh] = st
            d["states"].append(seen)
    for d in dirs:
        for hh, sl in enumerate(heads):
            inter = [lax.dot_general(d["qi"][C * ci:C * (ci + 1), sl], d["states"][hh][ci], _NT,
                                     preferred_element_type=F32) for ci in range(n)]
            d["o"][:, sl] = d["oi"][hh] + jnp.concatenate(inter, axis=0)


def _hgrn(hq, hv, kf, lff, kb, lfb, *, B, S, cb):
    T = B * S
    nblk = S // cb
    fwd = pl.BlockSpec((cb, HGRN_W), lambda b, i: (b * nblk + i, 0))
    bwd = pl.BlockSpec((cb, HGRN_W), lambda b, i: (b * nblk + nblk - 1 - i, 0))
    run = _const_spec((cb, cb))
    st = pltpu.VMEM((HGRN_HEADS, HGRN_DK, HGRN_DK), F32)
    return pl.pallas_call(
        _hgrn_kernel, grid=(B, nblk),
        in_specs=[fwd, fwd, fwd, fwd, bwd, bwd, bwd, bwd, run, run],
        out_specs=[fwd, bwd],
        out_shape=[jax.ShapeDtypeStruct((T, HGRN_W), F32)] * 2,
        scratch_shapes=[st, st],
        compiler_params=_params(("parallel", "arbitrary")), name="hgrn",
    )(hq, hv, kf, lff, hq, hv, kb, lfb, _hgrn_run_matrix(cb, False), _hgrn_run_matrix(cb, True))


MERGE_SUBTILES = 2

def _merge_kernel(ao_ref, of_ref, ob_ref, og_ref, sa_ref, sb_ref, x_ref, wa_ref, wb_ref, wo_ref,
                  gh_ref, gf_ref, wr_ref, x1_ref, h2_ref, lg_ref):
    tm = x_ref.shape[0]
    sub = tm // MERGE_SUBTILES
    rows = [slice(sub * s, sub * (s + 1)) for s in range(MERGE_SUBTILES)]
    ya = [_dot(ao_ref[r, :], wa_ref[...]) for r in rows]
    hn = []
    for r in rows:
        o = of_ref[r, :] + ob_ref[r, :]
        parts = []
        for hh in range(HGRN_HEADS):
            oh = o[:, HGRN_DK * hh:HGRN_DK * (hh + 1)]
            parts.append(oh * lax.rsqrt(jnp.mean(oh * oh, axis=1, keepdims=True) + EPS))
        hn.append((jnp.concatenate(parts, axis=1) * gh_ref[...] * og_ref[r, :].astype(F32)).astype(BF16))
    yb = [_dot(h, wb_ref[...]) for h in hn]
    merged = [(sa_ref[r, :].astype(F32) * a + sb_ref[r, :].astype(F32) * b).astype(BF16)
              for r, a, b in zip(rows, ya, yb)]
    x1 = [x_ref[r, :] + _dot(m, wo_ref[...]) for r, m in zip(rows, merged)]
    h2 = [v * lax.rsqrt(jnp.mean(v * v, axis=1, keepdims=True) + EPS) * gf_ref[...] for v in x1]
    hi = [v.astype(BF16) for v in h2]
    lo = [(v - h.astype(F32)).astype(BF16) for v, h in zip(h2, hi)]
    lg = [_dot(h, wr_ref[0]) + _dot(l, wr_ref[0]) + _dot(h, wr_ref[1]) for h, l in zip(hi, lo)]
    for s, r in enumerate(rows):
        x1_ref[r, :] = x1[s]
        lg_ref[r, :] = lg[s]
        _store_token_tiles(h2_ref.at[pl.ds(sub * ROW_TILES * s, sub * ROW_TILES), :], h2[s])


def _merge(ao, of, ob, og, sa, sb, x2d, wa, wb, wo, gh, gf, wr, *, tm):
    T = x2d.shape[0]
    tok = lambda w: pl.BlockSpec((tm, w), lambda i: (i, 0))
    return pl.pallas_call(
        _merge_kernel, grid=(T // tm,),
        in_specs=[tok(ATTN_Q_W), tok(HGRN_W), tok(HGRN_W), tok(HGRN_W), tok(D_MODEL), tok(D_MODEL), tok(D_MODEL),
                  _const_spec(wa.shape), _const_spec(wb.shape), _const_spec(wo.shape),
                  _const_spec(gh.shape), _const_spec(gf.shape), _const_spec(wr.shape)],
        out_specs=[tok(D_MODEL), pl.BlockSpec((tm * ROW_TILES, LANES), lambda i: (i, 0)), tok(ROUTE_W)],
        out_shape=[jax.ShapeDtypeStruct((T, D_MODEL), F32), jax.ShapeDtypeStruct((T * ROW_TILES, LANES), F32),
                   jax.ShapeDtypeStruct((T, ROUTE_W), F32)],
        compiler_params=_params(("parallel",)), name="merge",
    )(ao, of, ob, og, sa, sb, x2d, wa, wb, wo, gh, gf, wr)


def _first_argmax(vals, n):
    top = jnp.max(vals, axis=0, keepdims=True)
    rows = lax.broadcasted_iota(jnp.int32, vals.shape, 0)
    idx = jnp.min(jnp.where(vals == top, rows, n), axis=0, keepdims=True)
    return top, idx, rows


def _route_kernel(lg_ref, bias_ref, ids_ref, c1_ref, c2_ref):
    lt = lg_ref[...].T + bias_ref[...]
    gl = lt[0:N_GROUPS]
    ge = jnp.exp(gl - jnp.max(gl, axis=0, keepdims=True))
    gp = ge / jnp.sum(ge, axis=0, keepdims=True)
    p_g, g_idx, _ = _first_argmax(gp, N_GROUPS)
    esel = lt[EXPERT_ROW0:EXPERT_ROW0 + EPG]
    for gg in range(1, N_GROUPS):
        esel = jnp.where(g_idx == gg, lt[EXPERT_ROW0 + EPG * gg:EXPERT_ROW0 + EPG * (gg + 1)], esel)
    ee = jnp.exp(esel - jnp.max(esel, axis=0, keepdims=True))
    pe = ee / jnp.sum(ee, axis=0, keepdims=True)
    p1, i1, rows = _first_argmax(pe, EPG)
    p2, i2, _ = _first_argmax(jnp.where(rows == i1, -1.0, pe), EPG)
    den = p1 + p2
    c1 = p_g * (p1 / den)
    c2 = p_g * (p2 / den)
    e1 = g_idx * EPG + i1
    e2 = g_idx * EPG + i2
    tm = lt.shape[1]
    ids_ref[...] = jnp.concatenate([e1, e2, jnp.zeros((6, tm), jnp.int32)], axis=0)
    c1_ref[...] = jnp.broadcast_to(c1, (LANES, tm)).T
    c2_ref[...] = jnp.broadcast_to(c2, (LANES, tm)).T


def _route(lg, bias, *, tm):
    T = lg.shape[0]
    tok = pl.BlockSpec((tm, LANES), lambda i: (i, 0))
    return pl.pallas_call(
        _route_kernel, grid=(T // tm,),
        in_specs=[pl.BlockSpec((tm, ROUTE_W), lambda i: (i, 0)), _const_spec((ROUTE_W, 1))],
        out_specs=[pl.BlockSpec((8, tm), lambda i: (0, i)), tok, tok],
        out_shape=[jax.ShapeDtypeStruct((8, T), jnp.int32), jax.ShapeDtypeStruct((T, LANES), F32),
                   jax.ShapeDtypeStruct((T, LANES), F32)],
        compiler_params=_params(("parallel",)), name="route",
    )(lg, bias)


def _routing_tables(ids, tmx):
    T = ids.shape[1]
    P = TOP_K * T
    ef = ids.reshape(P)
    onehot = (ef[:, None] == jnp.arange(N_EXPERTS, dtype=jnp.int32)[None, :]).astype(jnp.int32)
    csum = jnp.cumsum(onehot, axis=0)
    cnt = csum[-1]
    rank = jnp.take_along_axis(csum, ef[:, None], axis=1)[:, 0] - 1
    padded = ((cnt + tmx - 1) // tmx) * tmx
    off = jnp.cumsum(padded) - padded
    pos = (off[ef] + rank).astype(jnp.int32)
    n_tiles = (P + N_EXPERTS * tmx) // tmx
    ends = off + padded
    tile_start = jnp.arange(n_tiles, dtype=jnp.int32) * tmx
    tile_e = jnp.sum((tile_start[:, None] >= ends[None, :]).astype(jnp.int32), axis=1)
    tile_valid = (tile_e < N_EXPERTS).astype(jnp.int32)
    tile_e = jnp.minimum(tile_e, N_EXPERTS - 1)
    used = jnp.sum(padded)
    pad_start = jnp.concatenate([off + cnt, used[None]]).astype(jnp.int32)
    n_pad = jnp.concatenate([padded - cnt, (n_tiles * tmx - used)[None]]).astype(jnp.int32)
    return pos.reshape(TOP_K, T), tile_e, tile_valid, pad_start, n_pad, n_tiles


def _dispatch_kernel(pad_start_ref, n_pad_ref, p1_ref, p2_ref, h_ref, xs_hbm, zero_ref, sem, zsem):
    i = pl.program_id(0)
    td = p1_ref.shape[2]

    @pl.when(i == 0)
    def _():
        zero_ref[...] = jnp.zeros(zero_ref.shape, F32)

        def per_range(e, carry):
            first = pad_start_ref[e]

            def fill(r, c):
                pltpu.make_async_copy(zero_ref, _row_tile(xs_hbm, first + r), zsem).start()
                return c

            def drain(r, c):
                pltpu.make_async_copy(zero_ref, _row_tile(xs_hbm, first), zsem).wait()
                return c

            lax.fori_loop(0, n_pad_ref[e], fill, 0)
            lax.fori_loop(0, n_pad_ref[e], drain, 0)
            return carry

        lax.fori_loop(0, pad_start_ref.shape[0], per_range, 0)

    def issue(r, carry):
        src = _row_tile(h_ref, r)
        pltpu.make_async_copy(src, _row_tile(xs_hbm, p1_ref[0, 0, r]), sem).start()
        pltpu.make_async_copy(src, _row_tile(xs_hbm, p2_ref[0, 0, r]), sem).start()
        return carry

    lax.fori_loop(0, td, issue, 0, unroll=8)
    for _ in range(TOP_K):
        pltpu.make_async_copy(h_ref, xs_hbm.at[pl.ds(0, td * ROW_TILES), :], sem).wait()


def _dispatch(pos, h2, pad_start, n_pad, n_rows, *, td):
    T = pos.shape[1]
    n = T // td
    idx = pl.BlockSpec((1, 1, td), lambda i, ps, npd: (i, 0, 0), memory_space=pltpu.SMEM)
    grid_spec = pltpu.PrefetchScalarGridSpec(
        num_scalar_prefetch=2, grid=(n,),
        in_specs=[idx, idx, pl.BlockSpec((td * ROW_TILES, LANES), lambda i, ps, npd: (i, 0))],
        out_specs=pl.BlockSpec(memory_space=pl.ANY),
        scratch_shapes=[pltpu.VMEM((ROW_TILES, LANES), F32), pltpu.SemaphoreType.DMA(()),
                        pltpu.SemaphoreType.DMA(())])
    return pl.pallas_call(
        _dispatch_kernel, grid_spec=grid_spec,
        out_shape=jax.ShapeDtypeStruct((n_rows * ROW_TILES, LANES), F32),
        compiler_params=_params(("arbitrary",)), name="dispatch",
    )(pad_start, n_pad, pos[0].reshape(n, 1, td), pos[1].reshape(n, 1, td), h2)


def _start_row_gather(idx_ref, src_hbm, dst, sem, n_rows):
    def issue(r, carry):
        pltpu.make_async_copy(_row_tile(src_hbm, idx_ref[0, 0, r]), _row_tile(dst, r), sem).start()
        return carry

    lax.fori_loop(0, n_rows, issue, 0, unroll=8)


def _wait_row_gather(src_hbm, dst, sem, n_rows):
    pltpu.make_async_copy(src_hbm.at[pl.ds(0, n_rows * ROW_TILES), :], dst, sem).wait()


def _moe_kernel(te_ref, tv_ref, x_ref, wg_ref, wu_ref, wd_ref, y_ref):
    i = pl.program_id(0)

    @pl.when(tv_ref[i] == 1)
    def _():
        x = _load_token_tiles(x_ref).astype(BF16)
        hid = _silu(_dot(x, wg_ref[0])) * _dot(x, wu_ref[0])
        _store_token_tiles(y_ref, _dot(hid.astype(BF16), wd_ref[0]))

    @pl.when(tv_ref[i] == 0)
    def _():
        y_ref[...] = jnp.zeros(y_ref.shape, F32)


def _moe(xs, tile_e, tile_valid, wg, wu, wd, *, tmx):
    n_tiles = tile_e.shape[0]
    rows = pl.BlockSpec((tmx * ROW_TILES, LANES), lambda i, te, tv: (i, 0))
    rows_in = pl.BlockSpec((tmx * ROW_TILES, LANES), lambda i, te, tv: (i * tv[i], 0))
    grid_spec = pltpu.PrefetchScalarGridSpec(
        num_scalar_prefetch=2, grid=(n_tiles,),
        in_specs=[rows_in,
                  pl.BlockSpec((1, D_MODEL, D_EXPERT), lambda i, te, tv: (te[i], 0, 0)),
                  pl.BlockSpec((1, D_MODEL, D_EXPERT), lambda i, te, tv: (te[i], 0, 0)),
                  pl.BlockSpec((1, D_EXPERT, D_MODEL), lambda i, te, tv: (te[i], 0, 0))],
        out_specs=rows)
    return pl.pallas_call(
        _moe_kernel, grid_spec=grid_spec,
        out_shape=jax.ShapeDtypeStruct((n_tiles * tmx * ROW_TILES, LANES), F32),
        compiler_params=_params(("arbitrary",)), name="moe",
    )(tile_e, tile_valid, xs, wg, wu, wd)


def _combine_kernel(p1c_ref, p2c_ref, p1n_ref, p2n_ref, x1_ref, c1_ref, c2_ref, y_hbm, o_ref, ybuf, sem):
    i = pl.program_id(0)
    n = pl.num_programs(0)
    tm = o_ref.shape[0]
    slot = i % 2

    def start(p1_ref, p2_ref, s):
        _start_row_gather(p1_ref, y_hbm, ybuf.at[s, 0], sem.at[s], tm)
        _start_row_gather(p2_ref, y_hbm, ybuf.at[s, 1], sem.at[s], tm)

    @pl.when(i == 0)
    def _():
        start(p1c_ref, p2c_ref, 0)

    @pl.when(i + 1 < n)
    def _():
        start(p1n_ref, p2n_ref, 1 - slot)

    _wait_row_gather(y_hbm, ybuf.at[slot, 0], sem.at[slot], tm)
    _wait_row_gather(y_hbm, ybuf.at[slot, 1], sem.at[slot], tm)
    reps = D_MODEL // LANES
    c1 = jnp.tile(c1_ref[...], (1, reps))
    c2 = jnp.tile(c2_ref[...], (1, reps))
    o_ref[...] = (x1_ref[...] + c1 * _load_token_tiles(ybuf.at[slot, 0])
                  + c2 * _load_token_tiles(ybuf.at[slot, 1]))


def _combine(pos, x1, c1, c2, y, *, tm):
    T = x1.shape[0]
    n = T // tm
    p1 = pos[0].reshape(n, 1, tm)
    p2 = pos[1].reshape(n, 1, tm)
    cur = pl.BlockSpec((1, 1, tm), lambda i: (i, 0, 0), memory_space=pltpu.SMEM)
    nxt = pl.BlockSpec((1, 1, tm), lambda i: (jnp.minimum(i + 1, n - 1), 0, 0), memory_space=pltpu.SMEM)
    tok = lambda w: pl.BlockSpec((tm, w), lambda i: (i, 0))
    return pl.pallas_call(
        _combine_kernel, grid=(n,),
        in_specs=[cur, cur, nxt, nxt, tok(D_MODEL), tok(LANES), tok(LANES), pl.BlockSpec(memory_space=pl.ANY)],
        out_specs=tok(D_MODEL),
        out_shape=jax.ShapeDtypeStruct((T, D_MODEL), F32),
        scratch_shapes=[pltpu.VMEM((2, TOP_K, tm * ROW_TILES, LANES), F32), pltpu.SemaphoreType.DMA((2,))],
        compiler_params=_params(("arbitrary",)), name="combine",
    )(p1, p2, p1, p2, x1, c1, c2, y)


def _tiles(S):
    return dict(tm=min(512, S), tq=min(256, S), cb=min(256, S), tmerge=min(512, S), troute=min(1024, S),
                tmoe=512, tcomb=256, tdisp=min(512, S))


def _rope_tables(S):
    rows = S // GRID_W
    row_ids = jnp.repeat(jnp.arange(rows), GRID_W).astype(F32)
    col_ids = jnp.tile(jnp.arange(GRID_W), rows).astype(F32)
    inv_freq = ROPE_THETA ** (-jnp.arange(0, HALF, 2, dtype=F32) / HALF)
    ang = jnp.concatenate([row_ids[:, None] * inv_freq, col_ids[:, None] * inv_freq], axis=-1)
    return jnp.cos(ang), jnp.sin(ang)


def kernel(x, g_mix, w_in, q_norm, k_norm, hgrn_norm, lb_fwd, lb_bwd, w_attn_branch, w_hgrn_branch, w_out, g_ffn,
           w_router_group, b_router_group, w_router_expert, b_router_expert, w_exp_gate, w_exp_up, w_exp_down):
    B, S, D = x.shape
    assert D == D_MODEL and w_in.shape == (1, D_MODEL, D_IN) and lb_fwd.shape[0] == 2
    T = B * S
    t = _tiles(S)

    perm = np.concatenate([np.arange(0, HEAD_DIM, 2), np.arange(1, HEAD_DIM, 2)])
    qcols = (np.arange(ATTN_HEADS)[:, None] * HEAD_DIM + perm[None, :]).reshape(-1)
    kcols = IN_OFF[1] + (np.arange(ATTN_KV_HEADS)[:, None] * HEAD_DIM + perm[None, :]).reshape(-1)
    cols = np.concatenate([qcols, kcols, np.arange(IN_OFF[2], D_IN)])
    w_all = w_in[0][:, cols].astype(BF16)
    gq = q_norm[0][perm][:, None].astype(F32)
    gk = k_norm[0][perm][:, None].astype(F32)
    cos, sin = _rope_tables(S)

    (qt, k, kn2, vt, hq, hv, kf, lff, kb, lfb, og, sa, sb) = _in_proj(
        x.reshape(T, D), g_mix.astype(F32), w_all, gq, gk, cos.T, sin.T,
        lb_fwd.astype(F32), lb_bwd.astype(F32), B=B, S=S, tm=t["tm"])

    ao = _attention(qt, k, kn2, vt, B=B, S=S, tq=t["tq"]).reshape(T, ATTN_Q_W)
    of, ob = _hgrn(hq, hv, kf, lff, kb, lfb, B=B, S=S, cb=t["cb"])

    wr = jnp.zeros((D_MODEL, ROUTE_W), F32)
    wr = wr.at[:, 0:N_GROUPS].set(w_router_group[0]).at[:, EXPERT_ROW0:EXPERT_ROW0 + N_EXPERTS].set(w_router_expert[0])
    wr_hi = wr.astype(BF16)
    wr_lo = (wr - wr_hi.astype(F32)).astype(BF16)
    rbias = jnp.zeros((ROUTE_W, 1), F32)
    rbias = rbias.at[0:N_GROUPS, 0].set(b_router_group[0]).at[EXPERT_ROW0:EXPERT_ROW0 + N_EXPERTS, 0].set(
        b_router_expert[0])
    x1, h2, lg = _merge(
        ao, of, ob, og, sa, sb, x.reshape(T, D), w_attn_branch[0].astype(BF16), w_hgrn_branch[0].astype(BF16),
        w_out[0].astype(BF16), jnp.tile(hgrn_norm[0], HGRN_HEADS)[None, :].astype(F32), g_ffn.astype(F32),
        jnp.stack([wr_hi, wr_lo]), tm=t["tmerge"])
    ids, c1, c2 = _route(lg, rbias, tm=t["troute"])
    pos, tile_e, tile_valid, pad_start, n_pad, n_tiles = _routing_tables(ids[0:TOP_K], t["tmoe"])
    xs = _dispatch(pos, h2, pad_start, n_pad, n_tiles * t["tmoe"], td=t["tdisp"])
    y = _moe(xs, tile_e, tile_valid, w_exp_gate[0].astype(BF16), w_exp_up[0].astype(BF16),
             w_exp_down[0].astype(BF16), tmx=t["tmoe"])
    out = _combine(pos, x1, c1, c2, y, tm=t["tcomb"])
    return out.reshape(B, S, D)
```

```python
import functools
import math

import jax
import jax.numpy as jnp
import numpy as np
from jax import lax
from jax.experimental import pallas as pl
from jax.experimental.pallas import tpu as pltpu

F32 = jnp.float32
BF16 = jnp.bfloat16

D_MODEL = 1024
GRID_W = 64
EPS = 1e-6
ATTN_HEADS = 8
ATTN_KV_HEADS = 2
GQA_GROUP = ATTN_HEADS // ATTN_KV_HEADS
HEAD_DIM = 64
HALF = HEAD_DIM // 2
ROPE_THETA = 10000.0
HGRN_HEADS = 4
HGRN_DK = 128
HGRN_CHUNK = 32
HGRN_SCALE = HGRN_DK ** -0.5
N_GROUPS = 4
EPG = 8
N_EXPERTS = N_GROUPS * EPG
TOP_K = 2
D_EXPERT = 512
ATTN_Q_W = ATTN_HEADS * HEAD_DIM
ATTN_KV_W = ATTN_KV_HEADS * HEAD_DIM
HGRN_W = HGRN_HEADS * HGRN_DK
IN_SPLITS = (ATTN_Q_W, ATTN_KV_W, ATTN_KV_W, HGRN_W, HGRN_W, HGRN_W, HGRN_W, HGRN_W, D_MODEL, D_MODEL)
IN_OFF = tuple(int(v) for v in np.cumsum((0,) + IN_SPLITS))
D_IN = IN_OFF[-1]
LANES = 128
ROUTE_W = LANES
EXPERT_ROW0 = 8
VMEM_LIMIT = 56 * 1024 * 1024
NEG_BIG = -1e30
LOG2E = math.log2(math.e)
SAFE_SCORE_BOUND = 60.0


def _sigmoid(x):
    return 1.0 / (1.0 + jnp.exp(-x))


def _silu(x):
    return x * _sigmoid(x)


def _dot(a, b):
    return jnp.dot(a, b, preferred_element_type=F32)


def _params(sem):
    return pltpu.CompilerParams(dimension_semantics=sem, vmem_limit_bytes=VMEM_LIMIT)


def _const_spec(shape):
    nd = len(shape)
    return pl.BlockSpec(shape, lambda *_: (0,) * nd)


ROW_TILES = D_MODEL // LANES


def _store_token_tiles(ref, val):
    rows = val.shape[0]
    for j in range(ROW_TILES):
        ref[pl.ds(j, rows, stride=ROW_TILES), :] = val[:, LANES * j:LANES * (j + 1)]


def _load_token_tiles(ref):
    rows = ref.shape[0] // ROW_TILES
    return jnp.concatenate([ref[pl.ds(j, rows, stride=ROW_TILES), :] for j in range(ROW_TILES)], axis=1)


def _row_tile(ref, r):
    return ref.at[pl.ds(pl.multiple_of(r * ROW_TILES, ROW_TILES), ROW_TILES), :]


INPROJ_SUBTILES = 2

def _inproj_kernel(x_ref, g_ref, w_ref, gq_ref, gk_ref, cos_ref, sin_ref, lbf_ref, lbb_ref,
                   qt_ref, k_ref, kn2_ref, vt_ref, hq_ref, hv_ref, kf_ref, lff_ref, kb_ref, lfb_ref,
                   og_ref, sa_ref, sb_ref):
    sub = x_ref.shape[0] // INPROJ_SUBTILES
    rows = [slice(sub * s, sub * (s + 1)) for s in range(INPROJ_SUBTILES)]
    hs = []
    for r in rows:
        x = x_ref[r, :]
        ms = jnp.mean(x * x, axis=-1, keepdims=True)
        hs.append((x * lax.rsqrt(ms + EPS) * g_ref[...]).astype(BF16))

    def proj(i, s):
        return _dot(hs[s], w_ref[:, IN_OFF[i]:IN_OFF[i + 1]])

    def norm_rope_t(y, n_heads, gain_ref, r):
        yt = y.T.reshape(n_heads, HEAD_DIM, sub)
        ss = jnp.sum(yt * yt, axis=1, keepdims=True)
        yn = yt * lax.rsqrt(ss * (1.0 / HEAD_DIM) + EPS) * gain_ref[...][None]
        a, b = yn[:, :HALF, :], yn[:, HALF:, :]
        c, s = cos_ref[:, r][None], sin_ref[:, r][None]
        return jnp.concatenate([a * c - b * s, a * s + b * c], axis=1)

    def lower_bound(lbraw_ref):
        raw = lbraw_ref[...]
        e = jnp.exp(raw - jnp.max(raw, axis=0, keepdims=True))
        return e[0:1] / jnp.sum(e, axis=0, keepdims=True)

    def direction(z, lb, k_out, lf_out, r):
        k_out[r, :] = ((1.0 - lb) * _sigmoid(-z)).astype(BF16)
        lf_out[r, :] = jnp.log(lb + (1.0 - lb) * _sigmoid(z))

    for s, r in enumerate(rows):
        qt_ref[0, :, :, r] = (norm_rope_t(proj(0, s), ATTN_HEADS, gq_ref, r)
                              * (HEAD_DIM ** -0.5 * LOG2E)).astype(BF16)
    for s, r in enumerate(rows):
        kt = norm_rope_t(proj(1, s), ATTN_KV_HEADS, gk_ref, r).astype(BF16).astype(F32)
        kn2_ref[0, :, r] = jnp.sum(kt * kt, axis=1)
        k = kt.reshape(ATTN_KV_W, sub).T
        for gg in range(ATTN_KV_HEADS):
            k_ref[0, gg, r, :] = k[:, HEAD_DIM * gg:HEAD_DIM * (gg + 1)].astype(BF16)
    for s, r in enumerate(rows):
        vt_ref[0, :, 0, :, r] = proj(2, s).T.reshape(ATTN_KV_HEADS, HEAD_DIM, sub).astype(BF16)
    for s, r in enumerate(rows):
        hq_ref[r, :] = (_silu(proj(3, s)) * HGRN_SCALE).astype(BF16)
    lb_f, lb_b = lower_bound(lbf_ref), lower_bound(lbb_ref)
    for s, r in enumerate(rows):
        direction(proj(4, s), lb_f, kf_ref, lff_ref, r)
    for s, r in enumerate(rows):
        direction(proj(5, s), lb_b, kb_ref, lfb_ref, r)
    for s, r in enumerate(rows):
        hv_ref[r, :] = proj(6, s).astype(BF16)
    for s, r in enumerate(rows):
        og_ref[r, :] = _silu(proj(7, s)).astype(BF16)
    for s, r in enumerate(rows):
        sa_ref[r, :] = _sigmoid(proj(8, s)).astype(BF16)
    for s, r in enumerate(rows):
        sb_ref[r, :] = _sigmoid(proj(9, s)).astype(BF16)


def _in_proj(x2d, g_mix, w_all, gq, gk, cos_t, sin_t, lb_fwd, lb_bwd, *, B, S, tm):
    T = B * S
    nsb = S // tm
    tok = lambda w: pl.BlockSpec((tm, w), lambda i: (i, 0))
    in_specs = [
        tok(D_MODEL), _const_spec((1, D_MODEL)), _const_spec((D_MODEL, D_IN)),
        _const_spec((HEAD_DIM, 1)), _const_spec((HEAD_DIM, 1)),
        pl.BlockSpec((HALF, tm), lambda i: (0, i % nsb)), pl.BlockSpec((HALF, tm), lambda i: (0, i % nsb)),
        _const_spec(lb_fwd.shape), _const_spec(lb_bwd.shape),
    ]
    out_shape = [
        jax.ShapeDtypeStruct((B, ATTN_HEADS, HEAD_DIM, S), BF16),
        jax.ShapeDtypeStruct((B, ATTN_KV_HEADS, S, HEAD_DIM), BF16),
        jax.ShapeDtypeStruct((B, ATTN_KV_HEADS, S), F32),
        jax.ShapeDtypeStruct((B, ATTN_KV_HEADS, nsb, HEAD_DIM, tm), BF16),
        jax.ShapeDtypeStruct((T, HGRN_W), BF16),
        jax.ShapeDtypeStruct((T, HGRN_W), BF16),
        jax.ShapeDtypeStruct((T, HGRN_W), BF16),
        jax.ShapeDtypeStruct((T, HGRN_W), F32),
        jax.ShapeDtypeStruct((T, HGRN_W), BF16),
        jax.ShapeDtypeStruct((T, HGRN_W), F32),
        jax.ShapeDtypeStruct((T, HGRN_W), BF16),
        jax.ShapeDtypeStruct((T, D_MODEL), BF16),
        jax.ShapeDtypeStruct((T, D_MODEL), BF16),
    ]
    out_specs = [
        pl.BlockSpec((1, ATTN_HEADS, HEAD_DIM, tm), lambda i: (i // nsb, 0, 0, i % nsb)),
        pl.BlockSpec((1, ATTN_KV_HEADS, tm, HEAD_DIM), lambda i: (i // nsb, 0, i % nsb, 0)),
        pl.BlockSpec((1, ATTN_KV_HEADS, tm), lambda i: (i // nsb, 0, i % nsb)),
        pl.BlockSpec((1, ATTN_KV_HEADS, 1, HEAD_DIM, tm), lambda i: (i // nsb, 0, i % nsb, 0, 0)),
        tok(HGRN_W), tok(HGRN_W), tok(HGRN_W), tok(HGRN_W), tok(HGRN_W), tok(HGRN_W), tok(HGRN_W),
        tok(D_MODEL), tok(D_MODEL),
    ]
    return pl.pallas_call(
        _inproj_kernel, grid=(T // tm,), in_specs=in_specs, out_specs=out_specs, out_shape=out_shape,
        compiler_params=_params(("parallel",)), name="in_proj",
    )(x2d, g_mix, w_all, gq, gk, cos_t, sin_t, lb_fwd, lb_bwd)


def _attn_kernel(qt_ref, k_ref, kn2_ref, vt_ref, o_ref, shift_ref, l_ref, acc_ref):
    g = pl.program_id(1)
    nk, tk = vt_ref.shape[2], vt_ref.shape[4]

    def keys(j):
        return k_ref[0, 0, pl.ds(pl.multiple_of(j * tk, tk), tk), :]

    kmax2 = jnp.max(kn2_ref[0, pl.ds(g, 1), :], axis=1, keepdims=True)
    for hh in range(GQA_GROUP):
        q = qt_ref[0, hh].astype(F32)
        shift_ref[hh] = jnp.sqrt(jnp.sum(q * q, axis=0, keepdims=True) * kmax2)

    @pl.when(jnp.max(shift_ref[...]) > SAFE_SCORE_BOUND)
    def _():
        shift_ref[...] = jnp.full(shift_ref.shape, NEG_BIG, F32)

        def max_body(j, carry):
            k = keys(j)
            for hh in range(GQA_GROUP):
                st = _dot(k, qt_ref[0, hh])
                shift_ref[hh] = jnp.maximum(shift_ref[hh], jnp.max(st, axis=0, keepdims=True))
            return carry

        lax.fori_loop(0, nk, max_body, 0)

    acc_ref[...] = jnp.zeros(acc_ref.shape, F32)
    l_ref[...] = jnp.zeros(l_ref.shape, F32)
    tq = qt_ref.shape[3]

    def body(j, carry):
        k = keys(j)
        vt = vt_ref[0, 0, j]
        sts = [_dot(k, qt_ref[0, hh]) for hh in range(GQA_GROUP)]
        for hh in range(GQA_GROUP):
            pt = jnp.exp2(sts[hh] - shift_ref[hh])
            l_ref[hh] += jnp.sum(pt.reshape(tk // 8, 8, tq), axis=0)
            acc_ref[hh] += _dot(vt, pt.astype(BF16))
        return carry

    lax.fori_loop(0, nk, body, 0, unroll=8)
    outs = [acc_ref[hh] * (1.0 / jnp.sum(l_ref[hh], axis=0, keepdims=True)) for hh in range(GQA_GROUP)]
    o_ref[0] = jnp.concatenate(outs, axis=0).T.astype(BF16)


def _attention(qt, k, kn2, vt, *, B, S, tq):
    nk, tk = vt.shape[2], vt.shape[4]
    gw = GQA_GROUP * HEAD_DIM
    return pl.pallas_call(
        _attn_kernel, grid=(B, ATTN_KV_HEADS, S // tq),
        in_specs=[
            pl.BlockSpec((1, GQA_GROUP, HEAD_DIM, tq), lambda b, g, i: (b, g, 0, i)),
            pl.BlockSpec((1, 1, S, HEAD_DIM), lambda b, g, i: (b, g, 0, 0)),
            pl.BlockSpec((1, ATTN_KV_HEADS, S), lambda b, g, i: (b, 0, 0)),
            pl.BlockSpec((1, 1, nk, HEAD_DIM, tk), lambda b, g, i: (b, g, 0, 0, 0)),
        ],
        out_specs=pl.BlockSpec((1, tq, gw), lambda b, g, i: (b, i, g)),
        out_shape=jax.ShapeDtypeStruct((B, S, ATTN_Q_W), BF16),
        scratch_shapes=[pltpu.VMEM((GQA_GROUP, 1, tq), F32), pltpu.VMEM((GQA_GROUP, 8, tq), F32),
                        pltpu.VMEM((GQA_GROUP, HEAD_DIM, tq), F32)],
        compiler_params=_params(("parallel", "parallel", "parallel")), name="attn",
    )(qt, k, kn2, vt)


def _hgrn_run_matrix(cb, reverse):
    C = HGRN_CHUNK
    t = np.arange(cb)[:, None]
    s = np.arange(cb)[None, :]
    same = (t // C) == (s // C)
    return jnp.asarray(same & ((s >= t) if reverse else (s <= t)), dtype=BF16)


_TN = (((0,), (0,)), ((), ()))
_NT = (((1,), (1,)), ((), ()))


def _hgrn_kernel(qf_ref, vf_ref, kf_ref, lff_ref, qb_ref, vb_ref, kb_ref, lfb_ref, runf_ref, runb_ref,
                 of_ref, ob_ref, stf_ref, stb_ref):
    @pl.when(pl.program_id(1) == 0)
    def _():
        stf_ref[...] = jnp.zeros(stf_ref.shape, F32)
        stb_ref[...] = jnp.zeros(stb_ref.shape, F32)

    cb = qf_ref.shape[0]
    C = HGRN_CHUNK
    n = cb // C
    row = lax.broadcasted_iota(jnp.int32, (cb, cb), 0)
    col = lax.broadcasted_iota(jnp.int32, (cb, cb), 1)
    same = (row // C) == (col // C)
    dirs = [
        dict(q=qf_ref, v=vf_ref, k=kf_ref, lf=lff_ref, run=runf_ref, o=of_ref, st=stf_ref, rev=False,
             mask=same & (col <= row), last=C - 1, mid=C // 2),
        dict(q=qb_ref, v=vb_ref, k=kb_ref, lf=lfb_ref, run=runb_ref, o=ob_ref, st=stb_ref, rev=True,
             mask=same & (col >= row), last=0, mid=C - 1 - C // 2),
    ]
    heads = [slice(HGRN_DK * hh, HGRN_DK * (hh + 1)) for hh in range(HGRN_HEADS)]

    for d in dirs:
        lf = d["lf"][...]
        lf_hi = lf.astype(BF16)
        lf_lo = (lf - lf_hi.astype(F32)).astype(BF16)
        d["b"] = _dot(d["run"][...], lf_hi) + _dot(d["run"][...], lf_lo)
    for d in dirs:
        b = d["b"]
        b3 = b.reshape(n, C, HGRN_W)
        b_last = jnp.broadcast_to(b3[:, d["last"]:d["last"] + 1, :], b3.shape).reshape(cb, HGRN_W)
        b_ref = jnp.broadcast_to(b3[:, d["mid"]:d["mid"] + 1, :], b3.shape).reshape(cb, HGRN_W)
        q = d["q"][...].astype(F32)
        k = d["k"][...].astype(F32)
        d["qs"] = (q * jnp.exp(b - b_ref)).astype(BF16)
        d["ks"] = (k * jnp.exp(b_ref - b)).astype(BF16)
        d["qi"] = (q * jnp.exp(b)).astype(BF16)
        d["kst"] = (k * jnp.exp(b_last - b)).astype(BF16)
        d["dec"] = jnp.exp(b_last)
        d["vv"] = d["v"][...]
    for d in dirs:
        d["a"] = [jnp.where(d["mask"], lax.dot_general(d["qs"][:, sl], d["ks"][:, sl], _NT,
                                                       preferred_element_type=F32), 0.0).astype(BF16)
                  for sl in heads]
    for d in dirs:
        d["oi"] = [_dot(d["a"][hh], d["vv"][:, sl]) for hh, sl in enumerate(heads)]
    for d in dirs:
        d["upd"] = [[lax.dot_general(d["vv"][C * ci:C * (ci + 1), sl], d["kst"][C * ci:C * (ci + 1), sl], _TN,
                                     preferred_element_type=F32) for ci in range(n)] for sl in heads]
    for d in dirs:
        order = range(n - 1, -1, -1) if d["rev"] else range(n)
        d["states"] = []
        for hh, sl in enumerate(heads):
            st = d["st"][hh]
            seen = [None] * n
            for ci in order:
                seen[ci] = st.astype(BF16)
                st = d["dec"][C * ci:C * ci + 1, sl] * st + d["upd"][hh][ci]
            d["st"][hh] = st
            d["states"].append(seen)
    for d in dirs:
        for hh, sl in enumerate(heads):
            inter = [lax.dot_general(d["qi"][C * ci:C * (ci + 1), sl], d["states"][hh][ci], _NT,
                                     preferred_element_type=F32) for ci in range(n)]
            d["o"][:, sl] = d["oi"][hh] + jnp.concatenate(inter, axis=0)


def _hgrn(hq, hv, kf, lff, kb, lfb, *, B, S, cb):
    T = B * S
    nblk = S // cb
    fwd = pl.BlockSpec((cb, HGRN_W), lambda b, i: (b * nblk + i, 0))
    bwd = pl.BlockSpec((cb, HGRN_W), lambda b, i: (b * nblk + nblk - 1 - i, 0))
    run = _const_spec((cb, cb))
    st = pltpu.VMEM((HGRN_HEADS, HGRN_DK, HGRN_DK), F32)
    return pl.pallas_call(
        _hgrn_kernel, grid=(B, nblk),
        in_specs=[fwd, fwd, fwd, fwd, bwd, bwd, bwd, bwd, run, run],
        out_specs=[fwd, bwd],
        out_shape=[jax.ShapeDtypeStruct((T, HGRN_W), F32)] * 2,
        scratch_shapes=[st, st],
        compiler_params=_params(("parallel", "arbitrary")), name="hgrn",
    )(hq, hv, kf, lff, hq, hv, kb, lfb, _hgrn_run_matrix(cb, False), _hgrn_run_matrix(cb, True))


MERGE_SUBTILES = 2

def _merge_kernel(ao_ref, of_ref, ob_ref, og_ref, sa_ref, sb_ref, x_ref, wa_ref, wb_ref, wo_ref,
                  gh_ref, gf_ref, wr_ref, x1_ref, h2_ref, lg_ref):
    tm = x_ref.shape[0]
    sub = tm // MERGE_SUBTILES
    rows = [slice(sub * s, sub * (s + 1)) for s in range(MERGE_SUBTILES)]
    ya = [_dot(ao_ref[r, :], wa_ref[...]) for r in rows]
    hn = []
    for r in rows:
        o = of_ref[r, :] + ob_ref[r, :]
        parts = []
        for hh in range(HGRN_HEADS):
            oh = o[:, HGRN_DK * hh:HGRN_DK * (hh + 1)]
            parts.append(oh * lax.rsqrt(jnp.mean(oh * oh, axis=1, keepdims=True) + EPS))
        hn.append((jnp.concatenate(parts, axis=1) * gh_ref[...] * og_ref[r, :].astype(F32)).astype(BF16))
    yb = [_dot(h, wb_ref[...]) for h in hn]
    merged = [(sa_ref[r, :].astype(F32) * a + sb_ref[r, :].astype(F32) * b).astype(BF16)
              for r, a, b in zip(rows, ya, yb)]
    x1 = [x_ref[r, :] + _dot(m, wo_ref[...]) for r, m in zip(rows, merged)]
    h2 = [v * lax.rsqrt(jnp.mean(v * v, axis=1, keepdims=True) + EPS) * gf_ref[...] for v in x1]
    hi = [v.astype(BF16) for v in h2]
    lo = [(v - h.astype(F32)).astype(BF16) for v, h in zip(h2, hi)]
    lg = [_dot(h, wr_ref[0]) + _dot(l, wr_ref[0]) + _dot(h, wr_ref[1]) for h, l in zip(hi, lo)]
    for s, r in enumerate(rows):
        x1_ref[r, :] = x1[s]
        lg_ref[r, :] = lg[s]
        _store_token_tiles(h2_ref.at[pl.ds(sub * ROW_TILES * s, sub * ROW_TILES), :], h2[s])


def _merge(ao, of, ob, og, sa, sb, x2d, wa, wb, wo, gh, gf, wr, *, tm):
    T = x2d.shape[0]
    tok = lambda w: pl.BlockSpec((tm, w), lambda i: (i, 0))
    return pl.pallas_call(
        _merge_kernel, grid=(T // tm,),
        in_specs=[tok(ATTN_Q_W), tok(HGRN_W), tok(HGRN_W), tok(HGRN_W), tok(D_MODEL), tok(D_MODEL), tok(D_MODEL),
                  _const_spec(wa.shape), _const_spec(wb.shape), _const_spec(wo.shape),
                  _const_spec(gh.shape), _const_spec(gf.shape), _const_spec(wr.shape)],
        out_specs=[tok(D_MODEL), pl.BlockSpec((tm * ROW_TILES, LANES), lambda i: (i, 0)), tok(ROUTE_W)],
        out_shape=[jax.ShapeDtypeStruct((T, D_MODEL), F32), jax.ShapeDtypeStruct((T * ROW_TILES, LANES), F32),
                   jax.ShapeDtypeStruct((T, ROUTE_W), F32)],
        compiler_params=_params(("parallel",)), name="merge",
    )(ao, of, ob, og, sa, sb, x2d, wa, wb, wo, gh, gf, wr)


def _first_argmax(vals, n):
    top = jnp.max(vals, axis=0, keepdims=True)
    rows = lax.broadcasted_iota(jnp.int32, vals.shape, 0)
    idx = jnp.min(jnp.where(vals == top, rows, n), axis=0, keepdims=True)
    return top, idx, rows


def _route_kernel(lg_ref, bias_ref, ids_ref, c1_ref, c2_ref):
    lt = lg_ref[...].T + bias_ref[...]
    gl = lt[0:N_GROUPS]
    ge = jnp.exp(gl - jnp.max(gl, axis=0, keepdims=True))
    gp = ge / jnp.sum(ge, axis=0, keepdims=True)
    p_g, g_idx, _ = _first_argmax(gp, N_GROUPS)
    esel = lt[EXPERT_ROW0:EXPERT_ROW0 + EPG]
    for gg in range(1, N_GROUPS):
        esel = jnp.where(g_idx == gg, lt[EXPERT_ROW0 + EPG * gg:EXPERT_ROW0 + EPG * (gg + 1)], esel)
    ee = jnp.exp(esel - jnp.max(esel, axis=0, keepdims=True))
    pe = ee / jnp.sum(ee, axis=0, keepdims=True)
    p1, i1, rows = _first_argmax(pe, EPG)
    p2, i2, _ = _first_argmax(jnp.where(rows == i1, -1.0, pe), EPG)
    den = p1 + p2
    c1 = p_g * (p1 / den)
    c2 = p_g * (p2 / den)
    e1 = g_idx * EPG + i1
    e2 = g_idx * EPG + i2
    tm = lt.shape[1]
    ids_ref[...] = jnp.concatenate([e1, e2, jnp.zeros((6, tm), jnp.int32)], axis=0)
    c1_ref[...] = jnp.broadcast_to(c1, (LANES, tm)).T
    c2_ref[...] = jnp.broadcast_to(c2, (LANES, tm)).T


def _route(lg, bias, *, tm):
    T = lg.shape[0]
    tok = pl.BlockSpec((tm, LANES), lambda i: (i, 0))
    return pl.pallas_call(
        _route_kernel, grid=(T // tm,),
        in_specs=[pl.BlockSpec((tm, ROUTE_W), lambda i: (i, 0)), _const_spec((ROUTE_W, 1))],
        out_specs=[pl.BlockSpec((8, tm), lambda i: (0, i)), tok, tok],
        out_shape=[jax.ShapeDtypeStruct((8, T), jnp.int32), jax.ShapeDtypeStruct((T, LANES), F32),
                   jax.ShapeDtypeStruct((T, LANES), F32)],
        compiler_params=_params(("parallel",)), name="route",
    )(lg, bias)


def _routing_tables(ids, tmx):
    T = ids.shape[1]
    P = TOP_K * T
    ef = ids.reshape(P)
    onehot = (ef[:, None] == jnp.arange(N_EXPERTS, dtype=jnp.int32)[None, :]).astype(jnp.int32)
    csum = jnp.cumsum(onehot, axis=0)
    cnt = csum[-1]
    rank = jnp.take_along_axis(csum, ef[:, None], axis=1)[:, 0] - 1
    padded = ((cnt + tmx - 1) // tmx) * tmx
    off = jnp.cumsum(padded) - padded
    pos = (off[ef] + rank).astype(jnp.int32)
    n_tiles = (P + N_EXPERTS * tmx) // tmx
    ends = off + padded
    tile_start = jnp.arange(n_tiles, dtype=jnp.int32) * tmx
    tile_e = jnp.sum((tile_start[:, None] >= ends[None, :]).astype(jnp.int32), axis=1)
    tile_valid = (tile_e < N_EXPERTS).astype(jnp.int32)
    tile_e = jnp.minimum(tile_e, N_EXPERTS - 1)
    used = jnp.sum(padded)
    pad_start = jnp.concatenate([off + cnt, used[None]]).astype(jnp.int32)
    n_pad = jnp.concatenate([padded - cnt, (n_tiles * tmx - used)[None]]).astype(jnp.int32)
    return pos.reshape(TOP_K, T), tile_e, tile_valid, pad_start, n_pad, n_tiles


def _dispatch_kernel(pad_start_ref, n_pad_ref, p1_ref, p2_ref, h_ref, xs_hbm, zero_ref, sem, zsem):
    i = pl.program_id(0)
    td = p1_ref.shape[2]

    @pl.when(i == 0)
    def _():
        zero_ref[...] = jnp.zeros(zero_ref.shape, F32)

        def per_range(e, carry):
            first = pad_start_ref[e]

            def fill(r, c):
                pltpu.make_async_copy(zero_ref, _row_tile(xs_hbm, first + r), zsem).start()
                return c

            def drain(r, c):
                pltpu.make_async_copy(zero_ref, _row_tile(xs_hbm, first), zsem).wait()
                return c

            lax.fori_loop(0, n_pad_ref[e], fill, 0)
            lax.fori_loop(0, n_pad_ref[e], drain, 0)
            return carry

        lax.fori_loop(0, pad_start_ref.shape[0], per_range, 0)

    def issue(r, carry):
        src = _row_tile(h_ref, r)
        pltpu.make_async_copy(src, _row_tile(xs_hbm, p1_ref[0, 0, r]), sem).start(priority=0)
        pltpu.make_async_copy(src, _row_tile(xs_hbm, p2_ref[0, 0, r]), sem).start(priority=1)
        return carry

    lax.fori_loop(0, td, issue, 0, unroll=8)
    for _ in range(TOP_K):
        pltpu.make_async_copy(h_ref, xs_hbm.at[pl.ds(0, td * ROW_TILES), :], sem).wait()


def _dispatch(pos, h2, pad_start, n_pad, n_rows, *, td):
    T = pos.shape[1]
    n = T // td
    idx = pl.BlockSpec((1, 1, td), lambda i, ps, npd: (i, 0, 0), memory_space=pltpu.SMEM)
    grid_spec = pltpu.PrefetchScalarGridSpec(
        num_scalar_prefetch=2, grid=(n,),
        in_specs=[idx, idx, pl.BlockSpec((td * ROW_TILES, LANES), lambda i, ps, npd: (i, 0))],
        out_specs=pl.BlockSpec(memory_space=pl.ANY),
        scratch_shapes=[pltpu.VMEM((ROW_TILES, LANES), F32), pltpu.SemaphoreType.DMA(()),
                        pltpu.SemaphoreType.DMA(())])
    return pl.pallas_call(
        _dispatch_kernel, grid_spec=grid_spec,
        out_shape=jax.ShapeDtypeStruct((n_rows * ROW_TILES, LANES), F32),
        compiler_params=_params(("arbitrary",)), name="dispatch",
    )(pad_start, n_pad, pos[0].reshape(n, 1, td), pos[1].reshape(n, 1, td), h2)


def _start_row_gather(idx_ref, src_hbm, dst, sem, n_rows):
    def issue(p, carry):
        for prio in range(2):
            r = 2 * p + prio
            pltpu.make_async_copy(_row_tile(src_hbm, idx_ref[0, 0, r]), _row_tile(dst, r), sem).start(priority=prio)
        return carry

    lax.fori_loop(0, n_rows // 2, issue, 0, unroll=4)


def _wait_row_gather(src_hbm, dst, sem, n_rows):
    pltpu.make_async_copy(src_hbm.at[pl.ds(0, n_rows * ROW_TILES), :], dst, sem).wait()


def _moe_kernel(te_ref, tv_ref, x_ref, wg_ref, wu_ref, wd_ref, y_ref, wg_s, wu_s, wd_s):
    i = pl.program_id(0)

    @pl.when((i == 0) | (te_ref[i] != te_ref[jnp.maximum(i - 1, 0)]))
    def _():
        wg_s[...] = wg_ref[0].astype(BF16)
        wu_s[...] = wu_ref[0].astype(BF16)
        wd_s[...] = wd_ref[0].astype(BF16)

    @pl.when(tv_ref[i] == 1)
    def _():
        x = _load_token_tiles(x_ref).astype(BF16)
        hid = _silu(_dot(x, wg_s[...])) * _dot(x, wu_s[...])
        _store_token_tiles(y_ref, _dot(hid.astype(BF16), wd_s[...]))

    @pl.when(tv_ref[i] == 0)
    def _():
        y_ref[...] = jnp.zeros(y_ref.shape, F32)


def _moe(xs, tile_e, tile_valid, wg, wu, wd, *, tmx):
    n_tiles = tile_e.shape[0]
    rows = pl.BlockSpec((tmx * ROW_TILES, LANES), lambda i, te, tv: (i, 0))
    rows_in = pl.BlockSpec((tmx * ROW_TILES, LANES), lambda i, te, tv: (i * tv[i], 0))
    grid_spec = pltpu.PrefetchScalarGridSpec(
        num_scalar_prefetch=2, grid=(n_tiles,),
        in_specs=[rows_in,
                  pl.BlockSpec((1, D_MODEL, D_EXPERT), lambda i, te, tv: (te[i], 0, 0)),
                  pl.BlockSpec((1, D_MODEL, D_EXPERT), lambda i, te, tv: (te[i], 0, 0)),
                  pl.BlockSpec((1, D_EXPERT, D_MODEL), lambda i, te, tv: (te[i], 0, 0))],
        out_specs=rows,
        scratch_shapes=[pltpu.VMEM((D_MODEL, D_EXPERT), BF16), pltpu.VMEM((D_MODEL, D_EXPERT), BF16),
                        pltpu.VMEM((D_EXPERT, D_MODEL), BF16)])
    return pl.pallas_call(
        _moe_kernel, grid_spec=grid_spec,
        out_shape=jax.ShapeDtypeStruct((n_tiles * tmx * ROW_TILES, LANES), F32),
        compiler_params=_params(("arbitrary",)), name="moe",
    )(tile_e, tile_valid, xs, wg, wu, wd)


def _combine_kernel(p1c_ref, p2c_ref, p1n_ref, p2n_ref, x1_ref, c1_ref, c2_ref, y_hbm, o_ref, ybuf, sem):
    i = pl.program_id(0)
    n = pl.num_programs(0)
    tm = o_ref.shape[0]
    slot = i % 2

    def start(p1_ref, p2_ref, s):
        _start_row_gather(p1_ref, y_hbm, ybuf.at[s, 0], sem.at[s], tm)
        _start_row_gather(p2_ref, y_hbm, ybuf.at[s, 1], sem.at[s], tm)

    @pl.when(i == 0)
    def _():
        start(p1c_ref, p2c_ref, 0)

    @pl.when(i + 1 < n)
    def _():
        start(p1n_ref, p2n_ref, 1 - slot)

    _wait_row_gather(y_hbm, ybuf.at[slot, 0], sem.at[slot], tm)
    _wait_row_gather(y_hbm, ybuf.at[slot, 1], sem.at[slot], tm)
    reps = D_MODEL // LANES
    c1 = jnp.tile(c1_ref[...], (1, reps))
    c2 = jnp.tile(c2_ref[...], (1, reps))
    o_ref[...] = (x1_ref[...] + c1 * _load_token_tiles(ybuf.at[slot, 0])
                  + c2 * _load_token_tiles(ybuf.at[slot, 1]))


def _combine(pos, x1, c1, c2, y, *, tm):
    T = x1.shape[0]
    n = T // tm
    p1 = pos[0].reshape(n, 1, tm)
    p2 = pos[1].reshape(n, 1, tm)
    cur = pl.BlockSpec((1, 1, tm), lambda i: (i, 0, 0), memory_space=pltpu.SMEM)
    nxt = pl.BlockSpec((1, 1, tm), lambda i: (jnp.minimum(i + 1, n - 1), 0, 0), memory_space=pltpu.SMEM)
    tok = lambda w: pl.BlockSpec((tm, w), lambda i: (i, 0))
    return pl.pallas_call(
        _combine_kernel, grid=(n,),
        in_specs=[cur, cur, nxt, nxt, tok(D_MODEL), tok(LANES), tok(LANES), pl.BlockSpec(memory_space=pl.ANY)],
        out_specs=tok(D_MODEL),
        out_shape=jax.ShapeDtypeStruct((T, D_MODEL), F32),
        scratch_shapes=[pltpu.VMEM((2, TOP_K, tm * ROW_TILES, LANES), F32), pltpu.SemaphoreType.DMA((2,))],
        compiler_params=_params(("arbitrary",)), name="combine",
    )(p1, p2, p1, p2, x1, c1, c2, y)


def _tiles(S):
    return dict(tm=min(512, S), tq=min(512, S), cb=min(256, S), tmerge=min(512, S), troute=min(1024, S),
                tmoe=512, tcomb=256, tdisp=min(512, S))


def _rope_tables(S):
    rows = S // GRID_W
    row_ids = jnp.repeat(jnp.arange(rows), GRID_W).astype(F32)
    col_ids = jnp.tile(jnp.arange(GRID_W), rows).astype(F32)
    inv_freq = ROPE_THETA ** (-jnp.arange(0, HALF, 2, dtype=F32) / HALF)
    ang = jnp.concatenate([row_ids[:, None] * inv_freq, col_ids[:, None] * inv_freq], axis=-1)
    return jnp.cos(ang), jnp.sin(ang)


def kernel(x, g_mix, w_in, q_norm, k_norm, hgrn_norm, lb_fwd, lb_bwd, w_attn_branch, w_hgrn_branch, w_out, g_ffn,
           w_router_group, b_router_group, w_router_expert, b_router_expert, w_exp_gate, w_exp_up, w_exp_down):
    B, S, D = x.shape
    assert D == D_MODEL and w_in.shape == (1, D_MODEL, D_IN) and lb_fwd.shape[0] == 2
    T = B * S
    t = _tiles(S)

    perm = np.concatenate([np.arange(0, HEAD_DIM, 2), np.arange(1, HEAD_DIM, 2)])
    qcols = (np.arange(ATTN_HEADS)[:, None] * HEAD_DIM + perm[None, :]).reshape(-1)
    kcols = IN_OFF[1] + (np.arange(ATTN_KV_HEADS)[:, None] * HEAD_DIM + perm[None, :]).reshape(-1)
    cols = np.concatenate([qcols, kcols, np.arange(IN_OFF[2], D_IN)])
    w_all = w_in[0][:, cols].astype(BF16)
    gq = q_norm[0][perm][:, None].astype(F32)
    gk = k_norm[0][perm][:, None].astype(F32)
    cos, sin = _rope_tables(S)

    (qt, k, kn2, vt, hq, hv, kf, lff, kb, lfb, og, sa, sb) = _in_proj(
        x.reshape(T, D), g_mix.astype(F32), w_all, gq, gk, cos.T, sin.T,
        lb_fwd.astype(F32), lb_bwd.astype(F32), B=B, S=S, tm=t["tm"])

    ao = _attention(qt, k, kn2, vt, B=B, S=S, tq=t["tq"]).reshape(T, ATTN_Q_W)
    of, ob = _hgrn(hq, hv, kf, lff, kb, lfb, B=B, S=S, cb=t["cb"])

    wr = jnp.zeros((D_MODEL, ROUTE_W), F32)
    wr = wr.at[:, 0:N_GROUPS].set(w_router_group[0]).at[:, EXPERT_ROW0:EXPERT_ROW0 + N_EXPERTS].set(w_router_expert[0])
    wr_hi = wr.astype(BF16)
    wr_lo = (wr - wr_hi.astype(F32)).astype(BF16)
    rbias = jnp.zeros((ROUTE_W, 1), F32)
    rbias = rbias.at[0:N_GROUPS, 0].set(b_router_group[0]).at[EXPERT_ROW0:EXPERT_ROW0 + N_EXPERTS, 0].set(
        b_router_expert[0])
    x1, h2, lg = _merge(
        ao, of, ob, og, sa, sb, x.reshape(T, D), w_attn_branch[0].astype(BF16), w_hgrn_branch[0].astype(BF16),
        w_out[0].astype(BF16), jnp.tile(hgrn_norm[0], HGRN_HEADS)[None, :].astype(F32), g_ffn.astype(F32),
        jnp.stack([wr_hi, wr_lo]), tm=t["tmerge"])
    ids, c1, c2 = _route(lg, rbias, tm=t["troute"])
    pos, tile_e, tile_valid, pad_start, n_pad, n_tiles = _routing_tables(ids[0:TOP_K], t["tmoe"])
    xs = _dispatch(pos, h2, pad_start, n_pad, n_tiles * t["tmoe"], td=t["tdisp"])
    y = _moe(xs, tile_e, tile_valid, w_exp_gate[0], w_exp_up[0], w_exp_down[0], tmx=t["tmoe"])
    out = _combine(pos, x1, c1, c2, y, tm=t["tcomb"])
    return out.reshape(B, S, D)
```

```python
import functools
import math

import jax
import jax.numpy as jnp
import numpy as np
from jax import lax
from jax.experimental import pallas as pl
from jax.experimental.pallas import tpu as pltpu

F32 = jnp.float32
BF16 = jnp.bfloat16

D_MODEL = 1024
GRID_W = 64
EPS = 1e-6
ATTN_HEADS = 8
ATTN_KV_HEADS = 2
GQA_GROUP = ATTN_HEADS // ATTN_KV_HEADS
HEAD_DIM = 64
HALF = HEAD_DIM // 2
ROPE_THETA = 10000.0
HGRN_HEADS = 4
HGRN_DK = 128
HGRN_CHUNK = 32
HGRN_SCALE = HGRN_DK ** -0.5
N_GROUPS = 4
EPG = 8
N_EXPERTS = N_GROUPS * EPG
TOP_K = 2
D_EXPERT = 512
ATTN_Q_W = ATTN_HEADS * HEAD_DIM
ATTN_KV_W = ATTN_KV_HEADS * HEAD_DIM
HGRN_W = HGRN_HEADS * HGRN_DK
IN_SPLITS = (ATTN_Q_W, ATTN_KV_W, ATTN_KV_W, HGRN_W, HGRN_W, HGRN_W, HGRN_W, HGRN_W, D_MODEL, D_MODEL)
IN_OFF = tuple(int(v) for v in np.cumsum((0,) + IN_SPLITS))
D_IN = IN_OFF[-1]
LANES = 128
ROUTE_W = LANES
EXPERT_ROW0 = 8
VMEM_LIMIT = 56 * 1024 * 1024
NEG_BIG = -1e30
LOG2E = math.log2(math.e)
SAFE_SCORE_BOUND = 60.0


def _sigmoid(x):
    return 1.0 / (1.0 + jnp.exp(-x))


def _silu(x):
    return x * _sigmoid(x)


def _dot(a, b):
    return jnp.dot(a, b, preferred_element_type=F32)


def _params(sem):
    return pltpu.CompilerParams(dimension_semantics=sem, vmem_limit_bytes=VMEM_LIMIT)


def _const_spec(shape):
    nd = len(shape)
    return pl.BlockSpec(shape, lambda *_: (0,) * nd)


ROW_TILES = D_MODEL // (2 * LANES)
ROW_DT = jnp.uint32


def _store_token_tiles(ref, val):
    rows = val.shape[0]
    for j in range(ROW_TILES):
        pair = [val[:, LANES * (2 * j + h):LANES * (2 * j + h + 1)] for h in range(2)]
        ref[pl.ds(j, rows, stride=ROW_TILES), :] = pltpu.pack_elementwise(pair, packed_dtype=jnp.bfloat16)


def _load_token_tiles(ref):
    rows = ref.shape[0] // ROW_TILES
    pieces = []
    for j in range(ROW_TILES):
        words = ref[pl.ds(j, rows, stride=ROW_TILES), :]
        pieces += [pltpu.unpack_elementwise(words, index=h, packed_dtype=jnp.bfloat16, unpacked_dtype=F32)
                   for h in range(2)]
    return jnp.concatenate(pieces, axis=1)


def _row_tile(ref, r):
    return ref.at[pl.ds(pl.multiple_of(r * ROW_TILES, ROW_TILES), ROW_TILES), :]


INPROJ_SUBTILES = 2

def _inproj_kernel(x_ref, g_ref, w_ref, gq_ref, gk_ref, cos_ref, sin_ref, lbf_ref, lbb_ref,
                   qt_ref, k_ref, kn2_ref, vt_ref, hq_ref, hv_ref, kf_ref, lff_ref, kb_ref, lfb_ref,
                   og_ref, sa_ref, sb_ref):
    sub = x_ref.shape[0] // INPROJ_SUBTILES
    rows = [slice(sub * s, sub * (s + 1)) for s in range(INPROJ_SUBTILES)]
    hs = []
    for r in rows:
        x = x_ref[r, :]
        ms = jnp.mean(x * x, axis=-1, keepdims=True)
        hs.append((x * lax.rsqrt(ms + EPS) * g_ref[...]).astype(BF16))

    def proj(i, s):
        return _dot(hs[s], w_ref[:, IN_OFF[i]:IN_OFF[i + 1]])

    def norm_rope_t(y, n_heads, gain_ref, r):
        yt = y.T.reshape(n_heads, HEAD_DIM, sub)
        ss = jnp.sum(yt * yt, axis=1, keepdims=True)
        yn = yt * lax.rsqrt(ss * (1.0 / HEAD_DIM) + EPS) * gain_ref[...][None]
        a, b = yn[:, :HALF, :], yn[:, HALF:, :]
        c, s = cos_ref[:, r][None], sin_ref[:, r][None]
        return jnp.concatenate([a * c - b * s, a * s + b * c], axis=1)

    def lower_bound(lbraw_ref):
        raw = lbraw_ref[...]
        e = jnp.exp(raw - jnp.max(raw, axis=0, keepdims=True))
        return e[0:1] / jnp.sum(e, axis=0, keepdims=True)

    def direction(z, lb, k_out, lf_out, r):
        k_out[r, :] = ((1.0 - lb) * _sigmoid(-z)).astype(BF16)
        lf_out[r, :] = jnp.log(lb + (1.0 - lb) * _sigmoid(z))

    for s, r in enumerate(rows):
        qt_ref[0, :, :, r] = (norm_rope_t(proj(0, s), ATTN_HEADS, gq_ref, r)
                              * (HEAD_DIM ** -0.5 * LOG2E)).astype(BF16)
    for s, r in enumerate(rows):
        kt = norm_rope_t(proj(1, s), ATTN_KV_HEADS, gk_ref, r).astype(BF16).astype(F32)
        kn2_ref[0, :, r] = jnp.sum(kt * kt, axis=1)
        k = kt.reshape(ATTN_KV_W, sub).T
        for gg in range(ATTN_KV_HEADS):
            k_ref[0, gg, r, :] = k[:, HEAD_DIM * gg:HEAD_DIM * (gg + 1)].astype(BF16)
    for s, r in enumerate(rows):
        vt_ref[0, :, 0, :, r] = proj(2, s).T.reshape(ATTN_KV_HEADS, HEAD_DIM, sub).astype(BF16)
    for s, r in enumerate(rows):
        hq_ref[r, :] = (_silu(proj(3, s)) * HGRN_SCALE).astype(BF16)
    lb_f, lb_b = lower_bound(lbf_ref), lower_bound(lbb_ref)
    for s, r in enumerate(rows):
        direction(proj(4, s), lb_f, kf_ref, lff_ref, r)
    for s, r in enumerate(rows):
        direction(proj(5, s), lb_b, kb_ref, lfb_ref, r)
    for s, r in enumerate(rows):
        hv_ref[r, :] = proj(6, s).astype(BF16)
    for s, r in enumerate(rows):
        og_ref[r, :] = _silu(proj(7, s)).astype(BF16)
    for s, r in enumerate(rows):
        sa_ref[r, :] = _sigmoid(proj(8, s)).astype(BF16)
    for s, r in enumerate(rows):
        sb_ref[r, :] = _sigmoid(proj(9, s)).astype(BF16)


def _in_proj(x2d, g_mix, w_all, gq, gk, cos_t, sin_t, lb_fwd, lb_bwd, *, B, S, tm):
    T = B * S
    nsb = S // tm
    tok = lambda w: pl.BlockSpec((tm, w), lambda i: (i, 0))
    in_specs = [
        tok(D_MODEL), _const_spec((1, D_MODEL)), _const_spec((D_MODEL, D_IN)),
        _const_spec((HEAD_DIM, 1)), _const_spec((HEAD_DIM, 1)),
        pl.BlockSpec((HALF, tm), lambda i: (0, i % nsb)), pl.BlockSpec((HALF, tm), lambda i: (0, i % nsb)),
        _const_spec(lb_fwd.shape), _const_spec(lb_bwd.shape),
    ]
    out_shape = [
        jax.ShapeDtypeStruct((B, ATTN_HEADS, HEAD_DIM, S), BF16),
        jax.ShapeDtypeStruct((B, ATTN_KV_HEADS, S, HEAD_DIM), BF16),
        jax.ShapeDtypeStruct((B, ATTN_KV_HEADS, S), F32),
        jax.ShapeDtypeStruct((B, ATTN_KV_HEADS, nsb, HEAD_DIM, tm), BF16),
        jax.ShapeDtypeStruct((T, HGRN_W), BF16),
        jax.ShapeDtypeStruct((T, HGRN_W), BF16),
        jax.ShapeDtypeStruct((T, HGRN_W), BF16),
        jax.ShapeDtypeStruct((T, HGRN_W), F32),
        jax.ShapeDtypeStruct((T, HGRN_W), BF16),
        jax.ShapeDtypeStruct((T, HGRN_W), F32),
        jax.ShapeDtypeStruct((T, HGRN_W), BF16),
        jax.ShapeDtypeStruct((T, D_MODEL), BF16),
        jax.ShapeDtypeStruct((T, D_MODEL), BF16),
    ]
    out_specs = [
        pl.BlockSpec((1, ATTN_HEADS, HEAD_DIM, tm), lambda i: (i // nsb, 0, 0, i % nsb)),
        pl.BlockSpec((1, ATTN_KV_HEADS, tm, HEAD_DIM), lambda i: (i // nsb, 0, i % nsb, 0)),
        pl.BlockSpec((1, ATTN_KV_HEADS, tm), lambda i: (i // nsb, 0, i % nsb)),
        pl.BlockSpec((1, ATTN_KV_HEADS, 1, HEAD_DIM, tm), lambda i: (i // nsb, 0, i % nsb, 0, 0)),
        tok(HGRN_W), tok(HGRN_W), tok(HGRN_W), tok(HGRN_W), tok(HGRN_W), tok(HGRN_W), tok(HGRN_W),
        tok(D_MODEL), tok(D_MODEL),
    ]
    return pl.pallas_call(
        _inproj_kernel, grid=(T // tm,), in_specs=in_specs, out_specs=out_specs, out_shape=out_shape,
        compiler_params=_params(("parallel",)), name="in_proj",
    )(x2d, g_mix, w_all, gq, gk, cos_t, sin_t, lb_fwd, lb_bwd)


def _attn_kernel(qt_ref, k_ref, kn2_ref, vt_ref, o_ref, shift_ref, l_ref, acc_ref):
    g = pl.program_id(1)
    nk, tk = vt_ref.shape[2], vt_ref.shape[4]

    def keys(j):
        return k_ref[0, 0, pl.ds(pl.multiple_of(j * tk, tk), tk), :]

    kmax2 = jnp.max(kn2_ref[0, pl.ds(g, 1), :], axis=1, keepdims=True)
    for hh in range(GQA_GROUP):
        q = qt_ref[0, hh].astype(F32)
        shift_ref[hh] = jnp.sqrt(jnp.sum(q * q, axis=0, keepdims=True) * kmax2)

    @pl.when(jnp.max(shift_ref[...]) > SAFE_SCORE_BOUND)
    def _():
        shift_ref[...] = jnp.full(shift_ref.shape, NEG_BIG, F32)

        def max_body(j, carry):
            k = keys(j)
            for hh in range(GQA_GROUP):
                st = _dot(k, qt_ref[0, hh])
                shift_ref[hh] = jnp.maximum(shift_ref[hh], jnp.max(st, axis=0, keepdims=True))
            return carry

        lax.fori_loop(0, nk, max_body, 0)

    acc_ref[...] = jnp.zeros(acc_ref.shape, F32)
    l_ref[...] = jnp.zeros(l_ref.shape, F32)
    tq = qt_ref.shape[3]

    def body(j, carry):
        k = keys(j)
        vt = vt_ref[0, 0, j]
        sts = [_dot(k, qt_ref[0, hh]) for hh in range(GQA_GROUP)]
        for hh in range(GQA_GROUP):
            pt = jnp.exp2(sts[hh] - shift_ref[hh])
            l_ref[hh] += jnp.sum(pt.reshape(tk // 8, 8, tq), axis=0)
            acc_ref[hh] += _dot(vt, pt.astype(BF16))
        return carry

    lax.fori_loop(0, nk, body, 0, unroll=8)
    outs = [acc_ref[hh] * (1.0 / jnp.sum(l_ref[hh], axis=0, keepdims=True)) for hh in range(GQA_GROUP)]
    o_ref[0] = jnp.concatenate(outs, axis=0).T.astype(BF16)


def _attention(qt, k, kn2, vt, *, B, S, tq):
    nk, tk = vt.shape[2], vt.shape[4]
    gw = GQA_GROUP * HEAD_DIM
    return pl.pallas_call(
        _attn_kernel, grid=(B, ATTN_KV_HEADS, S // tq),
        in_specs=[
            pl.BlockSpec((1, GQA_GROUP, HEAD_DIM, tq), lambda b, g, i: (b, g, 0, i)),
            pl.BlockSpec((1, 1, S, HEAD_DIM), lambda b, g, i: (b, g, 0, 0)),
            pl.BlockSpec((1, ATTN_KV_HEADS, S), lambda b, g, i: (b, 0, 0)),
            pl.BlockSpec((1, 1, nk, HEAD_DIM, tk), lambda b, g, i: (b, g, 0, 0, 0)),
        ],
        out_specs=pl.BlockSpec((1, tq, gw), lambda b, g, i: (b, i, g)),
        out_shape=jax.ShapeDtypeStruct((B, S, ATTN_Q_W), BF16),
        scratch_shapes=[pltpu.VMEM((GQA_GROUP, 1, tq), F32), pltpu.VMEM((GQA_GROUP, 8, tq), F32),
                        pltpu.VMEM((GQA_GROUP, HEAD_DIM, tq), F32)],
        compiler_params=_params(("parallel", "parallel", "parallel")), name="attn",
    )(qt, k, kn2, vt)


def _hgrn_run_matrix(cb, reverse):
    C = HGRN_CHUNK
    t = np.arange(cb)[:, None]
    s = np.arange(cb)[None, :]
    same = (t // C) == (s // C)
    return jnp.asarray(same & ((s >= t) if reverse else (s <= t)), dtype=BF16)


_TN = (((0,), (0,)), ((), ()))
_NT = (((1,), (1,)), ((), ()))


def _hgrn_kernel(qf_ref, vf_ref, kf_ref, lff_ref, qb_ref, vb_ref, kb_ref, lfb_ref, runf_ref, runb_ref,
                 of_ref, ob_ref, stf_ref, stb_ref):
    @pl.when(pl.program_id(1) == 0)
    def _():
        stf_ref[...] = jnp.zeros(stf_ref.shape, F32)
        stb_ref[...] = jnp.zeros(stb_ref.shape, F32)

    cb = qf_ref.shape[0]
    C = HGRN_CHUNK
    n = cb // C
    row = lax.broadcasted_iota(jnp.int32, (cb, cb), 0)
    col = lax.broadcasted_iota(jnp.int32, (cb, cb), 1)
    same = (row // C) == (col // C)
    dirs = [
        dict(q=qf_ref, v=vf_ref, k=kf_ref, lf=lff_ref, run=runf_ref, o=of_ref, st=stf_ref, rev=False,
             mask=same & (col <= row), last=C - 1, mid=C // 2),
        dict(q=qb_ref, v=vb_ref, k=kb_ref, lf=lfb_ref, run=runb_ref, o=ob_ref, st=stb_ref, rev=True,
             mask=same & (col >= row), last=0, mid=C - 1 - C // 2),
    ]
    heads = [slice(HGRN_DK * hh, HGRN_DK * (hh + 1)) for hh in range(HGRN_HEADS)]

    for d in dirs:
        lf = d["lf"][...]
        lf_hi = lf.astype(BF16)
        lf_lo = (lf - lf_hi.astype(F32)).astype(BF16)
        d["b"] = _dot(d["run"][...], lf_hi) + _dot(d["run"][...], lf_lo)
    for d in dirs:
        b = d["b"]
        b3 = b.reshape(n, C, HGRN_W)
        b_last = jnp.broadcast_to(b3[:, d["last"]:d["last"] + 1, :], b3.shape).reshape(cb, HGRN_W)
        b_ref = jnp.broadcast_to(b3[:, d["mid"]:d["mid"] + 1, :], b3.shape).reshape(cb, HGRN_W)
        q = d["q"][...].astype(F32)
        k = d["k"][...].astype(F32)
        d["qs"] = (q * jnp.exp(b - b_ref)).astype(BF16)
        d["ks"] = (k * jnp.exp(b_ref - b)).astype(BF16)
        d["qi"] = (q * jnp.exp(b)).astype(BF16)
        d["kst"] = (k * jnp.exp(b_last - b)).astype(BF16)
        d["dec"] = jnp.exp(b_last)
        d["vv"] = d["v"][...]
    for d in dirs:
        d["a"] = [jnp.where(d["mask"], lax.dot_general(d["qs"][:, sl], d["ks"][:, sl], _NT,
                                                       preferred_element_type=F32), 0.0).astype(BF16)
                  for sl in heads]
    for d in dirs:
        d["oi"] = [_dot(d["a"][hh], d["vv"][:, sl]) for hh, sl in enumerate(heads)]
    for d in dirs:
        d["upd"] = [[lax.dot_general(d["vv"][C * ci:C * (ci + 1), sl], d["kst"][C * ci:C * (ci + 1), sl], _TN,
                                     preferred_element_type=F32) for ci in range(n)] for sl in heads]
    for d in dirs:
        order = range(n - 1, -1, -1) if d["rev"] else range(n)
        d["states"] = []
        for hh, sl in enumerate(heads):
            st = d["st"][hh]
            seen = [None] * n
            for ci in order:
                seen[ci] = st.astype(BF16)
                st = d["dec"][C * ci:C * ci + 1, sl] * st + d["upd"][hh][ci]
            d["st"][hh] = st
            d["states"].append(seen)
    for d in dirs:
        for hh, sl in enumerate(heads):
            inter = [lax.dot_general(d["qi"][C * ci:C * (ci + 1), sl], d["states"][hh][ci], _NT,
                                     preferred_element_type=F32) for ci in range(n)]
            d["o"][:, sl] = d["oi"][hh] + jnp.concatenate(inter, axis=0)


def _hgrn(hq, hv, kf, lff, kb, lfb, *, B, S, cb):
    T = B * S
    nblk = S // cb
    fwd = pl.BlockSpec((cb, HGRN_W), lambda b, i: (b * nblk + i, 0))
    bwd = pl.BlockSpec((cb, HGRN_W), lambda b, i: (b * nblk + nblk - 1 - i, 0))
    run = _const_spec((cb, cb))
    st = pltpu.VMEM((HGRN_HEADS, HGRN_DK, HGRN_DK), F32)
    return pl.pallas_call(
        _hgrn_kernel, grid=(B, nblk),
        in_specs=[fwd, fwd, fwd, fwd, bwd, bwd, bwd, bwd, run, run],
        out_specs=[fwd, bwd],
        out_shape=[jax.ShapeDtypeStruct((T, HGRN_W), F32)] * 2,
        scratch_shapes=[st, st],
        compiler_params=_params(("parallel", "arbitrary")), name="hgrn",
    )(hq, hv, kf, lff, hq, hv, kb, lfb, _hgrn_run_matrix(cb, False), _hgrn_run_matrix(cb, True))


MERGE_SUBTILES = 2

def _merge_kernel(ao_ref, of_ref, ob_ref, og_ref, sa_ref, sb_ref, x_ref, wa_ref, wb_ref, wo_ref,
                  gh_ref, gf_ref, wr_ref, x1_ref, h2_ref, lg_ref):
    tm = x_ref.shape[0]
    sub = tm // MERGE_SUBTILES
    rows = [slice(sub * s, sub * (s + 1)) for s in range(MERGE_SUBTILES)]
    ya = [_dot(ao_ref[r, :], wa_ref[...]) for r in rows]
    hn = []
    for r in rows:
        o = of_ref[r, :] + ob_ref[r, :]
        parts = []
        for hh in range(HGRN_HEADS):
            oh = o[:, HGRN_DK * hh:HGRN_DK * (hh + 1)]
            parts.append(oh * lax.rsqrt(jnp.mean(oh * oh, axis=1, keepdims=True) + EPS))
        hn.append((jnp.concatenate(parts, axis=1) * gh_ref[...] * og_ref[r, :].astype(F32)).astype(BF16))
    yb = [_dot(h, wb_ref[...]) for h in hn]
    merged = [(sa_ref[r, :].astype(F32) * a + sb_ref[r, :].astype(F32) * b).astype(BF16)
              for r, a, b in zip(rows, ya, yb)]
    x1 = [x_ref[r, :] + _dot(m, wo_ref[...]) for r, m in zip(rows, merged)]
    h2 = [v * lax.rsqrt(jnp.mean(v * v, axis=1, keepdims=True) + EPS) * gf_ref[...] for v in x1]
    hi = [v.astype(BF16) for v in h2]
    lo = [(v - h.astype(F32)).astype(BF16) for v, h in zip(h2, hi)]
    lg = [_dot(h, wr_ref[0]) + _dot(l, wr_ref[0]) + _dot(h, wr_ref[1]) for h, l in zip(hi, lo)]
    for s, r in enumerate(rows):
        x1_ref[r, :] = x1[s]
        lg_ref[r, :] = lg[s]
        _store_token_tiles(h2_ref.at[pl.ds(sub * ROW_TILES * s, sub * ROW_TILES), :], h2[s])


def _merge(ao, of, ob, og, sa, sb, x2d, wa, wb, wo, gh, gf, wr, *, tm):
    T = x2d.shape[0]
    tok = lambda w: pl.BlockSpec((tm, w), lambda i: (i, 0))
    return pl.pallas_call(
        _merge_kernel, grid=(T // tm,),
        in_specs=[tok(ATTN_Q_W), tok(HGRN_W), tok(HGRN_W), tok(HGRN_W), tok(D_MODEL), tok(D_MODEL), tok(D_MODEL),
                  _const_spec(wa.shape), _const_spec(wb.shape), _const_spec(wo.shape),
                  _const_spec(gh.shape), _const_spec(gf.shape), _const_spec(wr.shape)],
        out_specs=[tok(D_MODEL), pl.BlockSpec((tm * ROW_TILES, LANES), lambda i: (i, 0)), tok(ROUTE_W)],
        out_shape=[jax.ShapeDtypeStruct((T, D_MODEL), F32), jax.ShapeDtypeStruct((T * ROW_TILES, LANES), ROW_DT),
                   jax.ShapeDtypeStruct((T, ROUTE_W), F32)],
        compiler_params=_params(("parallel",)), name="merge",
    )(ao, of, ob, og, sa, sb, x2d, wa, wb, wo, gh, gf, wr)


def _first_argmax(vals, n):
    top = jnp.max(vals, axis=0, keepdims=True)
    rows = lax.broadcasted_iota(jnp.int32, vals.shape, 0)
    idx = jnp.min(jnp.where(vals == top, rows, n), axis=0, keepdims=True)
    return top, idx, rows


def _route_kernel(lg_ref, bias_ref, ids_ref, c1_ref, c2_ref):
    lt = lg_ref[...].T + bias_ref[...]
    gl = lt[0:N_GROUPS]
    ge = jnp.exp(gl - jnp.max(gl, axis=0, keepdims=True))
    gp = ge / jnp.sum(ge, axis=0, keepdims=True)
    p_g, g_idx, _ = _first_argmax(gp, N_GROUPS)
    esel = lt[EXPERT_ROW0:EXPERT_ROW0 + EPG]
    for gg in range(1, N_GROUPS):
        esel = jnp.where(g_idx == gg, lt[EXPERT_ROW0 + EPG * gg:EXPERT_ROW0 + EPG * (gg + 1)], esel)
    ee = jnp.exp(esel - jnp.max(esel, axis=0, keepdims=True))
    pe = ee / jnp.sum(ee, axis=0, keepdims=True)
    p1, i1, rows = _first_argmax(pe, EPG)
    p2, i2, _ = _first_argmax(jnp.where(rows == i1, -1.0, pe), EPG)
    den = p1 + p2
    c1 = p_g * (p1 / den)
    c2 = p_g * (p2 / den)
    e1 = g_idx * EPG + i1
    e2 = g_idx * EPG + i2
    tm = lt.shape[1]
    ids_ref[...] = jnp.concatenate([e1, e2, jnp.zeros((6, tm), jnp.int32)], axis=0)
    c1_ref[...] = jnp.broadcast_to(c1, (LANES, tm)).T
    c2_ref[...] = jnp.broadcast_to(c2, (LANES, tm)).T


def _route(lg, bias, *, tm):
    T = lg.shape[0]
    tok = pl.BlockSpec((tm, LANES), lambda i: (i, 0))
    return pl.pallas_call(
        _route_kernel, grid=(T // tm,),
        in_specs=[pl.BlockSpec((tm, ROUTE_W), lambda i: (i, 0)), _const_spec((ROUTE_W, 1))],
        out_specs=[pl.BlockSpec((8, tm), lambda i: (0, i)), tok, tok],
        out_shape=[jax.ShapeDtypeStruct((8, T), jnp.int32), jax.ShapeDtypeStruct((T, LANES), F32),
                   jax.ShapeDtypeStruct((T, LANES), F32)],
        compiler_params=_params(("parallel",)), name="route",
    )(lg, bias)


def _routing_tables(ids, tmx):
    T = ids.shape[1]
    P = TOP_K * T
    ef = ids.reshape(P)
    onehot = (ef[:, None] == jnp.arange(N_EXPERTS, dtype=jnp.int32)[None, :]).astype(jnp.int32)
    csum = jnp.cumsum(onehot, axis=0)
    cnt = csum[-1]
    rank = jnp.take_along_axis(csum, ef[:, None], axis=1)[:, 0] - 1
    padded = ((cnt + tmx - 1) // tmx) * tmx
    off = jnp.cumsum(padded) - padded
    pos = (off[ef] + rank).astype(jnp.int32)
    n_tiles = (P + N_EXPERTS * tmx) // tmx
    ends = off + padded
    tile_start = jnp.arange(n_tiles, dtype=jnp.int32) * tmx
    tile_e = jnp.sum((tile_start[:, None] >= ends[None, :]).astype(jnp.int32), axis=1)
    tile_valid = (tile_e < N_EXPERTS).astype(jnp.int32)
    tile_e = jnp.minimum(tile_e, N_EXPERTS - 1)
    used = jnp.sum(padded)
    pad_start = jnp.concatenate([off + cnt, used[None]]).astype(jnp.int32)
    n_pad = jnp.concatenate([padded - cnt, (n_tiles * tmx - used)[None]]).astype(jnp.int32)
    return pos.reshape(TOP_K, T), tile_e, tile_valid, pad_start, n_pad, n_tiles


def _dispatch_kernel(pad_start_ref, n_pad_ref, p1_ref, p2_ref, h_ref, xs_hbm, zero_ref, sem, zsem):
    i = pl.program_id(0)
    td = p1_ref.shape[2]

    @pl.when(i == 0)
    def _():
        zero_ref[...] = jnp.zeros(zero_ref.shape, ROW_DT)

        def per_range(e, carry):
            first = pad_start_ref[e]

            def fill(r, c):
                pltpu.make_async_copy(zero_ref, _row_tile(xs_hbm, first + r), zsem).start()
                return c

            def drain(r, c):
                pltpu.make_async_copy(zero_ref, _row_tile(xs_hbm, first), zsem).wait()
                return c

            lax.fori_loop(0, n_pad_ref[e], fill, 0)
            lax.fori_loop(0, n_pad_ref[e], drain, 0)
            return carry

        lax.fori_loop(0, pad_start_ref.shape[0], per_range, 0)

    def issue(r, carry):
        src = _row_tile(h_ref, r)
        pltpu.make_async_copy(src, _row_tile(xs_hbm, p1_ref[0, 0, r]), sem).start(priority=0)
        pltpu.make_async_copy(src, _row_tile(xs_hbm, p2_ref[0, 0, r]), sem).start(priority=1)
        return carry

    lax.fori_loop(0, td, issue, 0, unroll=8)
    for _ in range(TOP_K):
        pltpu.make_async_copy(h_ref, xs_hbm.at[pl.ds(0, td * ROW_TILES), :], sem).wait()


def _dispatch(pos, h2, pad_start, n_pad, n_rows, *, td):
    T = pos.shape[1]
    n = T // td
    idx = pl.BlockSpec((1, 1, td), lambda i, ps, npd: (i, 0, 0), memory_space=pltpu.SMEM)
    grid_spec = pltpu.PrefetchScalarGridSpec(
        num_scalar_prefetch=2, grid=(n,),
        in_specs=[idx, idx, pl.BlockSpec((td * ROW_TILES, LANES), lambda i, ps, npd: (i, 0))],
        out_specs=pl.BlockSpec(memory_space=pl.ANY),
        scratch_shapes=[pltpu.VMEM((ROW_TILES, LANES), ROW_DT), pltpu.SemaphoreType.DMA(()),
                        pltpu.SemaphoreType.DMA(())])
    return pl.pallas_call(
        _dispatch_kernel, grid_spec=grid_spec,
        out_shape=jax.ShapeDtypeStruct((n_rows * ROW_TILES, LANES), ROW_DT),
        compiler_params=_params(("arbitrary",)), name="dispatch",
    )(pad_start, n_pad, pos[0].reshape(n, 1, td), pos[1].reshape(n, 1, td), h2)


def _start_row_gather(idx_ref, src_hbm, dst, sem, n_rows):
    def issue(p, carry):
        for prio in range(2):
            r = 2 * p + prio
            pltpu.make_async_copy(_row_tile(src_hbm, idx_ref[0, 0, r]), _row_tile(dst, r), sem).start(priority=prio)
        return carry

    lax.fori_loop(0, n_rows // 2, issue, 0, unroll=4)


def _wait_row_gather(src_hbm, dst, sem, n_rows):
    pltpu.make_async_copy(src_hbm.at[pl.ds(0, n_rows * ROW_TILES), :], dst, sem).wait()


def _moe_kernel(te_ref, tv_ref, x_ref, wg_ref, wu_ref, wd_ref, y_ref, wg_s, wu_s, wd_s):
    i = pl.program_id(0)

    @pl.when((i == 0) | (te_ref[i] != te_ref[jnp.maximum(i - 1, 0)]))
    def _():
        wg_s[...] = wg_ref[0].astype(BF16)
        wu_s[...] = wu_ref[0].astype(BF16)
        wd_s[...] = wd_ref[0].astype(BF16)

    @pl.when(tv_ref[i] == 1)
    def _():
        x = _load_token_tiles(x_ref).astype(BF16)
        hid = _silu(_dot(x, wg_s[...])) * _dot(x, wu_s[...])
        _store_token_tiles(y_ref, _dot(hid.astype(BF16), wd_s[...]))

    @pl.when(tv_ref[i] == 0)
    def _():
        y_ref[...] = jnp.zeros(y_ref.shape, ROW_DT)


def _moe(xs, tile_e, tile_valid, wg, wu, wd, *, tmx):
    n_tiles = tile_e.shape[0]
    rows = pl.BlockSpec((tmx * ROW_TILES, LANES), lambda i, te, tv: (i, 0))
    rows_in = pl.BlockSpec((tmx * ROW_TILES, LANES), lambda i, te, tv: (i * tv[i], 0))
    grid_spec = pltpu.PrefetchScalarGridSpec(
        num_scalar_prefetch=2, grid=(n_tiles,),
        in_specs=[rows_in,
                  pl.BlockSpec((1, D_MODEL, D_EXPERT), lambda i, te, tv: (te[i], 0, 0)),
                  pl.BlockSpec((1, D_MODEL, D_EXPERT), lambda i, te, tv: (te[i], 0, 0)),
                  pl.BlockSpec((1, D_EXPERT, D_MODEL), lambda i, te, tv: (te[i], 0, 0))],
        out_specs=rows,
        scratch_shapes=[pltpu.VMEM((D_MODEL, D_EXPERT), BF16), pltpu.VMEM((D_MODEL, D_EXPERT), BF16),
                        pltpu.VMEM((D_EXPERT, D_MODEL), BF16)])
    return pl.pallas_call(
        _moe_kernel, grid_spec=grid_spec,
        out_shape=jax.ShapeDtypeStruct((n_tiles * tmx * ROW_TILES, LANES), ROW_DT),
        compiler_params=_params(("arbitrary",)), name="moe",
    )(tile_e, tile_valid, xs, wg, wu, wd)


def _combine_kernel(p1c_ref, p2c_ref, p1n_ref, p2n_ref, x1_ref, c1_ref, c2_ref, y_hbm, o_ref, ybuf, sem):
    i = pl.program_id(0)
    n = pl.num_programs(0)
    tm = o_ref.shape[0]
    slot = i % 2

    def start(p1_ref, p2_ref, s):
        _start_row_gather(p1_ref, y_hbm, ybuf.at[s, 0], sem.at[s], tm)
        _start_row_gather(p2_ref, y_hbm, ybuf.at[s, 1], sem.at[s], tm)

    @pl.when(i == 0)
    def _():
        start(p1c_ref, p2c_ref, 0)

    @pl.when(i + 1 < n)
    def _():
        start(p1n_ref, p2n_ref, 1 - slot)

    _wait_row_gather(y_hbm, ybuf.at[slot, 0], sem.at[slot], tm)
    _wait_row_gather(y_hbm, ybuf.at[slot, 1], sem.at[slot], tm)
    reps = D_MODEL // LANES
    c1 = jnp.tile(c1_ref[...], (1, reps))
    c2 = jnp.tile(c2_ref[...], (1, reps))
    o_ref[...] = (x1_ref[...] + c1 * _load_token_tiles(ybuf.at[slot, 0])
                  + c2 * _load_token_tiles(ybuf.at[slot, 1]))


def _combine(pos, x1, c1, c2, y, *, tm):
    T = x1.shape[0]
    n = T // tm
    p1 = pos[0].reshape(n, 1, tm)
    p2 = pos[1].reshape(n, 1, tm)
    cur = pl.BlockSpec((1, 1, tm), lambda i: (i, 0, 0), memory_space=pltpu.SMEM)
    nxt = pl.BlockSpec((1, 1, tm), lambda i: (jnp.minimum(i + 1, n - 1), 0, 0), memory_space=pltpu.SMEM)
    tok = lambda w: pl.BlockSpec((tm, w), lambda i: (i, 0))
    return pl.pallas_call(
        _combine_kernel, grid=(n,),
        in_specs=[cur, cur, nxt, nxt, tok(D_MODEL), tok(LANES), tok(LANES), pl.BlockSpec(memory_space=pl.ANY)],
        out_specs=tok(D_MODEL),
        out_shape=jax.ShapeDtypeStruct((T, D_MODEL), F32),
        scratch_shapes=[pltpu.VMEM((2, TOP_K, tm * ROW_TILES, LANES), ROW_DT), pltpu.SemaphoreType.DMA((2,))],
        compiler_params=_params(("arbitrary",)), name="combine",
    )(p1, p2, p1, p2, x1, c1, c2, y)


def _tiles(S):
    return dict(tm=min(512, S), tq=min(512, S), cb=min(256, S), tmerge=min(512, S), troute=min(1024, S),
                tmoe=512, tcomb=256, tdisp=min(512, S))


def _rope_tables(S):
    rows = S // GRID_W
    row_ids = jnp.repeat(jnp.arange(rows), GRID_W).astype(F32)
    col_ids = jnp.tile(jnp.arange(GRID_W), rows).astype(F32)
    inv_freq = ROPE_THETA ** (-jnp.arange(0, HALF, 2, dtype=F32) / HALF)
    ang = jnp.concatenate([row_ids[:, None] * inv_freq, col_ids[:, None] * inv_freq], axis=-1)
    return jnp.cos(ang), jnp.sin(ang)


def kernel(x, g_mix, w_in, q_norm, k_norm, hgrn_norm, lb_fwd, lb_bwd, w_attn_branch, w_hgrn_branch, w_out, g_ffn,
           w_router_group, b_router_group, w_router_expert, b_router_expert, w_exp_gate, w_exp_up, w_exp_down):
    B, S, D = x.shape
    assert D == D_MODEL and w_in.shape == (1, D_MODEL, D_IN) and lb_fwd.shape[0] == 2
    T = B * S
    t = _tiles(S)

    perm = np.concatenate([np.arange(0, HEAD_DIM, 2), np.arange(1, HEAD_DIM, 2)])
    qcols = (np.arange(ATTN_HEADS)[:, None] * HEAD_DIM + perm[None, :]).reshape(-1)
    kcols = IN_OFF[1] + (np.arange(ATTN_KV_HEADS)[:, None] * HEAD_DIM + perm[None, :]).reshape(-1)
    cols = np.concatenate([qcols, kcols, np.arange(IN_OFF[2], D_IN)])
    w_all = w_in[0][:, cols].astype(BF16)
    gq = q_norm[0][perm][:, None].astype(F32)
    gk = k_norm[0][perm][:, None].astype(F32)
    cos, sin = _rope_tables(S)

    (qt, k, kn2, vt, hq, hv, kf, lff, kb, lfb, og, sa, sb) = _in_proj(
        x.reshape(T, D), g_mix.astype(F32), w_all, gq, gk, cos.T, sin.T,
        lb_fwd.astype(F32), lb_bwd.astype(F32), B=B, S=S, tm=t["tm"])

    ao = _attention(qt, k, kn2, vt, B=B, S=S, tq=t["tq"]).reshape(T, ATTN_Q_W)
    of, ob = _hgrn(hq, hv, kf, lff, kb, lfb, B=B, S=S, cb=t["cb"])

    wr = jnp.zeros((D_MODEL, ROUTE_W), F32)
    wr = wr.at[:, 0:N_GROUPS].set(w_router_group[0]).at[:, EXPERT_ROW0:EXPERT_ROW0 + N_EXPERTS].set(w_router_expert[0])
    wr_hi = wr.astype(BF16)
    wr_lo = (wr - wr_hi.astype(F32)).astype(BF16)
    rbias = jnp.zeros((ROUTE_W, 1), F32)
    rbias = rbias.at[0:N_GROUPS, 0].set(b_router_group[0]).at[EXPERT_ROW0:EXPERT_ROW0 + N_EXPERTS, 0].set(
        b_router_expert[0])
    x1, h2, lg = _merge(
        ao, of, ob, og, sa, sb, x.reshape(T, D), w_attn_branch[0].astype(BF16), w_hgrn_branch[0].astype(BF16),
        w_out[0].astype(BF16), jnp.tile(hgrn_norm[0], HGRN_HEADS)[None, :].astype(F32), g_ffn.astype(F32),
        jnp.stack([wr_hi, wr_lo]), tm=t["tmerge"])
    ids, c1, c2 = _route(lg, rbias, tm=t["troute"])
    pos, tile_e, tile_valid, pad_start, n_pad, n_tiles = _routing_tables(ids[0:TOP_K], t["tmoe"])
    xs = _dispatch(pos, h2, pad_start, n_pad, n_tiles * t["tmoe"], td=t["tdisp"])
    y = _moe(xs, tile_e, tile_valid, w_exp_gate[0], w_exp_up[0], w_exp_down[0], tmx=t["tmoe"])
    out = _combine(pos, x1, c1, c2, y, tm=t["tcomb"])
    return out.reshape(B, S, D)
```

```python
import functools
import math

import jax
import jax.numpy as jnp
import numpy as np
from jax import lax
from jax.experimental import pallas as pl
from jax.experimental.pallas import tpu as pltpu

F32 = jnp.float32
BF16 = jnp.bfloat16

D_MODEL = 1024
GRID_W = 64
EPS = 1e-6
ATTN_HEADS = 8
ATTN_KV_HEADS = 2
GQA_GROUP = ATTN_HEADS // ATTN_KV_HEADS
HEAD_DIM = 64
HALF = HEAD_DIM // 2
ROPE_THETA = 10000.0
HGRN_HEADS = 4
HGRN_DK = 128
HGRN_CHUNK = 32
HGRN_SCALE = HGRN_DK ** -0.5
N_GROUPS = 4
EPG = 8
N_EXPERTS = N_GROUPS * EPG
TOP_K = 2
D_EXPERT = 512
ATTN_Q_W = ATTN_HEADS * HEAD_DIM
ATTN_KV_W = ATTN_KV_HEADS * HEAD_DIM
HGRN_W = HGRN_HEADS * HGRN_DK
IN_SPLITS = (ATTN_Q_W, ATTN_KV_W, ATTN_KV_W, HGRN_W, HGRN_W, HGRN_W, HGRN_W, HGRN_W, D_MODEL, D_MODEL)
IN_OFF = tuple(int(v) for v in np.cumsum((0,) + IN_SPLITS))
D_IN = IN_OFF[-1]
LANES = 128
ROUTE_W = LANES
EXPERT_ROW0 = 8
VMEM_LIMIT = 56 * 1024 * 1024
NEG_BIG = -1e30
LOG2E = math.log2(math.e)
SAFE_SCORE_BOUND = 60.0


def _sigmoid(x):
    return 1.0 / (1.0 + jnp.exp(-x))


def _silu(x):
    return x * _sigmoid(x)


def _dot(a, b):
    return jnp.dot(a, b, preferred_element_type=F32)


def _params(sem):
    return pltpu.CompilerParams(dimension_semantics=sem, vmem_limit_bytes=VMEM_LIMIT)


def _const_spec(shape):
    nd = len(shape)
    return pl.BlockSpec(shape, lambda *_: (0,) * nd)


ROW_TILES = D_MODEL // LANES
ROW_DT = F32


def _store_token_tiles(ref, val):
    rows = val.shape[0]
    for j in range(ROW_TILES):
        ref[pl.ds(j, rows, stride=ROW_TILES), :] = val[:, LANES * j:LANES * (j + 1)]


def _load_token_tiles(ref):
    rows = ref.shape[0] // ROW_TILES
    return jnp.concatenate([ref[pl.ds(j, rows, stride=ROW_TILES), :] for j in range(ROW_TILES)], axis=1)


def _row_tile(ref, r):
    return ref.at[pl.ds(pl.multiple_of(r * ROW_TILES, ROW_TILES), ROW_TILES), :]


INPROJ_SUBTILES = 4

def _inproj_kernel(x_ref, g_ref, w_ref, gq_ref, gk_ref, cos_ref, sin_ref, lbf_ref, lbb_ref,
                   qt_ref, k_ref, kn2_ref, vt_ref, hq_ref, hv_ref, kf_ref, lff_ref, kb_ref, lfb_ref,
                   og_ref, sa_ref, sb_ref):
    sub = x_ref.shape[0] // INPROJ_SUBTILES
    rows = [slice(sub * s, sub * (s + 1)) for s in range(INPROJ_SUBTILES)]
    hs = []
    for r in rows:
        x = x_ref[r, :]
        ms = jnp.mean(x * x, axis=-1, keepdims=True)
        hs.append((x * lax.rsqrt(ms + EPS) * g_ref[...]).astype(BF16))

    def proj(i, s):
        return _dot(hs[s], w_ref[:, IN_OFF[i]:IN_OFF[i + 1]])

    def norm_rope_t(y, n_heads, gain_ref, r):
        yt = y.T.reshape(n_heads, HEAD_DIM, sub)
        ss = jnp.sum(yt * yt, axis=1, keepdims=True)
        yn = yt * lax.rsqrt(ss * (1.0 / HEAD_DIM) + EPS) * gain_ref[...][None]
        a, b = yn[:, :HALF, :], yn[:, HALF:, :]
        c, s = cos_ref[:, r][None], sin_ref[:, r][None]
        return jnp.concatenate([a * c - b * s, a * s + b * c], axis=1)

    def lower_bound(lbraw_ref):
        raw = lbraw_ref[...]
        e = jnp.exp(raw - jnp.max(raw, axis=0, keepdims=True))
        return e[0:1] / jnp.sum(e, axis=0, keepdims=True)

    def direction(z, lb, k_out, lf_out, r):
        k_out[r, :] = ((1.0 - lb) * _sigmoid(-z)).astype(BF16)
        lf_out[r, :] = jnp.log(lb + (1.0 - lb) * _sigmoid(z))

    for s, r in enumerate(rows):
        qt_ref[0, :, :, r] = (norm_rope_t(proj(0, s), ATTN_HEADS, gq_ref, r)
                              * (HEAD_DIM ** -0.5 * LOG2E)).astype(BF16)
    for s, r in enumerate(rows):
        kt = norm_rope_t(proj(1, s), ATTN_KV_HEADS, gk_ref, r).astype(BF16).astype(F32)
        kn2_ref[0, :, r] = jnp.sum(kt * kt, axis=1)
        k = kt.reshape(ATTN_KV_W, sub).T
        for gg in range(ATTN_KV_HEADS):
            k_ref[0, gg, r, :] = k[:, HEAD_DIM * gg:HEAD_DIM * (gg + 1)].astype(BF16)
    for s, r in enumerate(rows):
        vt_ref[0, :, 0, :, r] = proj(2, s).T.reshape(ATTN_KV_HEADS, HEAD_DIM, sub).astype(BF16)
    for s, r in enumerate(rows):
        hq_ref[r, :] = (_silu(proj(3, s)) * HGRN_SCALE).astype(BF16)
    lb_f, lb_b = lower_bound(lbf_ref), lower_bound(lbb_ref)
    for s, r in enumerate(rows):
        direction(proj(4, s), lb_f, kf_ref, lff_ref, r)
    for s, r in enumerate(rows):
        direction(proj(5, s), lb_b, kb_ref, lfb_ref, r)
    for s, r in enumerate(rows):
        hv_ref[r, :] = proj(6, s).astype(BF16)
    for s, r in enumerate(rows):
        og_ref[r, :] = _silu(proj(7, s)).astype(BF16)
    for s, r in enumerate(rows):
        sa_ref[r, :] = _sigmoid(proj(8, s)).astype(BF16)
    for s, r in enumerate(rows):
        sb_ref[r, :] = _sigmoid(proj(9, s)).astype(BF16)


def _in_proj(x2d, g_mix, w_all, gq, gk, cos_t, sin_t, lb_fwd, lb_bwd, *, B, S, tm):
    T = B * S
    nsb = S // tm
    tok = lambda w: pl.BlockSpec((tm, w), lambda i: (i, 0))
    in_specs = [
        tok(D_MODEL), _const_spec((1, D_MODEL)), _const_spec((D_MODEL, D_IN)),
        _const_spec((HEAD_DIM, 1)), _const_spec((HEAD_DIM, 1)),
        pl.BlockSpec((HALF, tm), lambda i: (0, i % nsb)), pl.BlockSpec((HALF, tm), lambda i: (0, i % nsb)),
        _const_spec(lb_fwd.shape), _const_spec(lb_bwd.shape),
    ]
    out_shape = [
        jax.ShapeDtypeStruct((B, ATTN_HEADS, HEAD_DIM, S), BF16),
        jax.ShapeDtypeStruct((B, ATTN_KV_HEADS, S, HEAD_DIM), BF16),
        jax.ShapeDtypeStruct((B, ATTN_KV_HEADS, S), F32),
        jax.ShapeDtypeStruct((B, ATTN_KV_HEADS, nsb, HEAD_DIM, tm), BF16),
        jax.ShapeDtypeStruct((T, HGRN_W), BF16),
        jax.ShapeDtypeStruct((T, HGRN_W), BF16),
        jax.ShapeDtypeStruct((T, HGRN_W), BF16),
        jax.ShapeDtypeStruct((T, HGRN_W), F32),
        jax.ShapeDtypeStruct((T, HGRN_W), BF16),
        jax.ShapeDtypeStruct((T, HGRN_W), F32),
        jax.ShapeDtypeStruct((T, HGRN_W), BF16),
        jax.ShapeDtypeStruct((T, D_MODEL), BF16),
        jax.ShapeDtypeStruct((T, D_MODEL), BF16),
    ]
    out_specs = [
        pl.BlockSpec((1, ATTN_HEADS, HEAD_DIM, tm), lambda i: (i // nsb, 0, 0, i % nsb)),
        pl.BlockSpec((1, ATTN_KV_HEADS, tm, HEAD_DIM), lambda i: (i // nsb, 0, i % nsb, 0)),
        pl.BlockSpec((1, ATTN_KV_HEADS, tm), lambda i: (i // nsb, 0, i % nsb)),
        pl.BlockSpec((1, ATTN_KV_HEADS, 1, HEAD_DIM, tm), lambda i: (i // nsb, 0, i % nsb, 0, 0)),
        tok(HGRN_W), tok(HGRN_W), tok(HGRN_W), tok(HGRN_W), tok(HGRN_W), tok(HGRN_W), tok(HGRN_W),
        tok(D_MODEL), tok(D_MODEL),
    ]
    return pl.pallas_call(
        _inproj_kernel, grid=(T // tm,), in_specs=in_specs, out_specs=out_specs, out_shape=out_shape,
        compiler_params=_params(("parallel",)), name="in_proj",
    )(x2d, g_mix, w_all, gq, gk, cos_t, sin_t, lb_fwd, lb_bwd)


def _attn_kernel(qt_ref, k_ref, kn2_ref, vt_ref, o_ref, shift_ref, l_ref, acc_ref):
    g = pl.program_id(1)
    nk, tk = vt_ref.shape[2], vt_ref.shape[4]

    def keys(j):
        return k_ref[0, 0, pl.ds(pl.multiple_of(j * tk, tk), tk), :]

    kmax2 = jnp.max(kn2_ref[0, pl.ds(g, 1), :], axis=1, keepdims=True)
    for hh in range(GQA_GROUP):
        q = qt_ref[0, hh].astype(F32)
        shift_ref[hh] = jnp.sqrt(jnp.sum(q * q, axis=0, keepdims=True) * kmax2)

    @pl.when(jnp.max(shift_ref[...]) > SAFE_SCORE_BOUND)
    def _():
        shift_ref[...] = jnp.full(shift_ref.shape, NEG_BIG, F32)

        def max_body(j, carry):
            k = keys(j)
            for hh in range(GQA_GROUP):
                st = _dot(k, qt_ref[0, hh])
                shift_ref[hh] = jnp.maximum(shift_ref[hh], jnp.max(st, axis=0, keepdims=True))
            return carry

        lax.fori_loop(0, nk, max_body, 0)

    acc_ref[...] = jnp.zeros(acc_ref.shape, F32)
    l_ref[...] = jnp.zeros(l_ref.shape, F32)
    tq = qt_ref.shape[3]

    def body(j, carry):
        k = keys(j)
        vt = vt_ref[0, 0, j]
        sts = [_dot(k, qt_ref[0, hh]) for hh in range(GQA_GROUP)]
        for hh in range(GQA_GROUP):
            pt = jnp.exp2(sts[hh] - shift_ref[hh])
            l_ref[hh] += jnp.sum(pt.reshape(tk // 8, 8, tq), axis=0)
            acc_ref[hh] += _dot(vt, pt.astype(BF16))
        return carry

    lax.fori_loop(0, nk, body, 0, unroll=8)
    outs = [acc_ref[hh] * (1.0 / jnp.sum(l_ref[hh], axis=0, keepdims=True)) for hh in range(GQA_GROUP)]
    o_ref[0] = jnp.concatenate(outs, axis=0).T.astype(BF16)


def _attention(qt, k, kn2, vt, *, B, S, tq):
    nk, tk = vt.shape[2], vt.shape[4]
    gw = GQA_GROUP * HEAD_DIM
    return pl.pallas_call(
        _attn_kernel, grid=(B, ATTN_KV_HEADS, S // tq),
        in_specs=[
            pl.BlockSpec((1, GQA_GROUP, HEAD_DIM, tq), lambda b, g, i: (b, g, 0, i)),
            pl.BlockSpec((1, 1, S, HEAD_DIM), lambda b, g, i: (b, g, 0, 0)),
            pl.BlockSpec((1, ATTN_KV_HEADS, S), lambda b, g, i: (b, 0, 0)),
            pl.BlockSpec((1, 1, nk, HEAD_DIM, tk), lambda b, g, i: (b, g, 0, 0, 0)),
        ],
        out_specs=pl.BlockSpec((1, tq, gw), lambda b, g, i: (b, i, g)),
        out_shape=jax.ShapeDtypeStruct((B, S, ATTN_Q_W), BF16),
        scratch_shapes=[pltpu.VMEM((GQA_GROUP, 1, tq), F32), pltpu.VMEM((GQA_GROUP, 8, tq), F32),
                        pltpu.VMEM((GQA_GROUP, HEAD_DIM, tq), F32)],
        compiler_params=_params(("parallel", "parallel", "parallel")), name="attn",
    )(qt, k, kn2, vt)


def _hgrn_run_matrix(cb, reverse):
    C = HGRN_CHUNK
    t = np.arange(cb)[:, None]
    s = np.arange(cb)[None, :]
    same = (t // C) == (s // C)
    return jnp.asarray(same & ((s >= t) if reverse else (s <= t)), dtype=BF16)


_TN = (((0,), (0,)), ((), ()))
_NT = (((1,), (1,)), ((), ()))


def _hgrn_kernel(qf_ref, vf_ref, kf_ref, lff_ref, qb_ref, vb_ref, kb_ref, lfb_ref, runf_ref, runb_ref,
                 of_ref, ob_ref, stf_ref, stb_ref):
    @pl.when(pl.program_id(1) == 0)
    def _():
        stf_ref[...] = jnp.zeros(stf_ref.shape, F32)
        stb_ref[...] = jnp.zeros(stb_ref.shape, F32)

    cb = qf_ref.shape[0]
    C = HGRN_CHUNK
    n = cb // C
    row = lax.broadcasted_iota(jnp.int32, (cb, cb), 0)
    col = lax.broadcasted_iota(jnp.int32, (cb, cb), 1)
    same = (row // C) == (col // C)
    dirs = [
        dict(q=qf_ref, v=vf_ref, k=kf_ref, lf=lff_ref, run=runf_ref, o=of_ref, st=stf_ref, rev=False,
             mask=same & (col <= row), last=C - 1, mid=C // 2),
        dict(q=qb_ref, v=vb_ref, k=kb_ref, lf=lfb_ref, run=runb_ref, o=ob_ref, st=stb_ref, rev=True,
             mask=same & (col >= row), last=0, mid=C - 1 - C // 2),
    ]
    heads = [slice(HGRN_DK * hh, HGRN_DK * (hh + 1)) for hh in range(HGRN_HEADS)]

    for d in dirs:
        lf = d["lf"][...]
        lf_hi = lf.astype(BF16)
        lf_lo = (lf - lf_hi.astype(F32)).astype(BF16)
        d["b"] = _dot(d["run"][...], lf_hi) + _dot(d["run"][...], lf_lo)
    for d in dirs:
        b = d["b"]
        b3 = b.reshape(n, C, HGRN_W)
        b_last = jnp.broadcast_to(b3[:, d["last"]:d["last"] + 1, :], b3.shape).reshape(cb, HGRN_W)
        b_ref = jnp.broadcast_to(b3[:, d["mid"]:d["mid"] + 1, :], b3.shape).reshape(cb, HGRN_W)
        q = d["q"][...].astype(F32)
        k = d["k"][...].astype(F32)
        d["qs"] = (q * jnp.exp(b - b_ref)).astype(BF16)
        d["ks"] = (k * jnp.exp(b_ref - b)).astype(BF16)
        d["qi"] = (q * jnp.exp(b)).astype(BF16)
        d["kst"] = (k * jnp.exp(b_last - b)).astype(BF16)
        d["dec"] = jnp.exp(b_last)
        d["vv"] = d["v"][...]
    for d in dirs:
        d["a"] = [jnp.where(d["mask"], lax.dot_general(d["qs"][:, sl], d["ks"][:, sl], _NT,
                                                       preferred_element_type=F32), 0.0).astype(BF16)
                  for sl in heads]
    for d in dirs:
        d["oi"] = [_dot(d["a"][hh], d["vv"][:, sl]) for hh, sl in enumerate(heads)]
    for d in dirs:
        d["upd"] = [[lax.dot_general(d["vv"][C * ci:C * (ci + 1), sl], d["kst"][C * ci:C * (ci + 1), sl], _TN,
                                     preferred_element_type=F32) for ci in range(n)] for sl in heads]
    for d in dirs:
        order = range(n - 1, -1, -1) if d["rev"] else range(n)
        d["states"] = []
        for hh, sl in enumerate(heads):
            st = d["st"][hh]
            seen = [None] * n
            for ci in order:
                seen[ci] = st.astype(BF16)
                st = d["dec"][C * ci:C * ci + 1, sl] * st + d["upd"][hh][ci]
            d["st"][hh] = st
            d["states"].append(seen)
    for d in dirs:
        for hh, sl in enumerate(heads):
            inter = [lax.dot_general(d["qi"][C * ci:C * (ci + 1), sl], d["states"][hh][ci], _NT,
                                     preferred_element_type=F32) for ci in range(n)]
            d["o"][:, sl] = d["oi"][hh] + jnp.concatenate(inter, axis=0)


def _hgrn(hq, hv, kf, lff, kb, lfb, *, B, S, cb):
    T = B * S
    nblk = S // cb
    fwd = pl.BlockSpec((cb, HGRN_W), lambda b, i: (b * nblk + i, 0))
    bwd = pl.BlockSpec((cb, HGRN_W), lambda b, i: (b * nblk + nblk - 1 - i, 0))
    run = _const_spec((cb, cb))
    st = pltpu.VMEM((HGRN_HEADS, HGRN_DK, HGRN_DK), F32)
    return pl.pallas_call(
        _hgrn_kernel, grid=(B, nblk),
        in_specs=[fwd, fwd, fwd, fwd, bwd, bwd, bwd, bwd, run, run],
        out_specs=[fwd, bwd],
        out_shape=[jax.ShapeDtypeStruct((T, HGRN_W), F32)] * 2,
        scratch_shapes=[st, st],
        compiler_params=_params(("parallel", "arbitrary")), name="hgrn",
    )(hq, hv, kf, lff, hq, hv, kb, lfb, _hgrn_run_matrix(cb, False), _hgrn_run_matrix(cb, True))


MERGE_SUBTILES = 2

def _merge_kernel(ao_ref, of_ref, ob_ref, og_ref, sa_ref, sb_ref, x_ref, wa_ref, wb_ref, wo_ref,
                  gh_ref, gf_ref, wr_ref, x1_ref, h2_ref, lg_ref):
    tm = x_ref.shape[0]
    sub = tm // MERGE_SUBTILES
    rows = [slice(sub * s, sub * (s + 1)) for s in range(MERGE_SUBTILES)]
    ya = [_dot(ao_ref[r, :], wa_ref[...]) for r in rows]
    hn = []
    for r in rows:
        o = of_ref[r, :] + ob_ref[r, :]
        parts = []
        for hh in range(HGRN_HEADS):
            oh = o[:, HGRN_DK * hh:HGRN_DK * (hh + 1)]
            parts.append(oh * lax.rsqrt(jnp.mean(oh * oh, axis=1, keepdims=True) + EPS))
        hn.append((jnp.concatenate(parts, axis=1) * gh_ref[...] * og_ref[r, :].astype(F32)).astype(BF16))
    yb = [_dot(h, wb_ref[...]) for h in hn]
    merged = [(sa_ref[r, :].astype(F32) * a + sb_ref[r, :].astype(F32) * b).astype(BF16)
              for r, a, b in zip(rows, ya, yb)]
    x1 = [x_ref[r, :] + _dot(m, wo_ref[...]) for r, m in zip(rows, merged)]
    h2 = [v * lax.rsqrt(jnp.mean(v * v, axis=1, keepdims=True) + EPS) * gf_ref[...] for v in x1]
    hi = [v.astype(BF16) for v in h2]
    lo = [(v - h.astype(F32)).astype(BF16) for v, h in zip(h2, hi)]
    lg = [_dot(h, wr_ref[...]) + _dot(l, wr_ref[...]) for h, l in zip(hi, lo)]
    for s, r in enumerate(rows):
        x1_ref[r, :] = x1[s]
        lg_ref[r, :] = lg[s]
        _store_token_tiles(h2_ref.at[pl.ds(sub * ROW_TILES * s, sub * ROW_TILES), :], h2[s])


def _merge(ao, of, ob, og, sa, sb, x2d, wa, wb, wo, gh, gf, wr, *, tm):
    T = x2d.shape[0]
    tok = lambda w: pl.BlockSpec((tm, w), lambda i: (i, 0))
    return pl.pallas_call(
        _merge_kernel, grid=(T // tm,),
        in_specs=[tok(ATTN_Q_W), tok(HGRN_W), tok(HGRN_W), tok(HGRN_W), tok(D_MODEL), tok(D_MODEL), tok(D_MODEL),
                  _const_spec(wa.shape), _const_spec(wb.shape), _const_spec(wo.shape),
                  _const_spec(gh.shape), _const_spec(gf.shape), _const_spec(wr.shape)],
        out_specs=[tok(D_MODEL), pl.BlockSpec((tm * ROW_TILES, LANES), lambda i: (i, 0)), tok(ROUTE_W)],
        out_shape=[jax.ShapeDtypeStruct((T, D_MODEL), F32), jax.ShapeDtypeStruct((T * ROW_TILES, LANES), ROW_DT),
                   jax.ShapeDtypeStruct((T, ROUTE_W), F32)],
        compiler_params=_params(("parallel",)), name="merge",
    )(ao, of, ob, og, sa, sb, x2d, wa, wb, wo, gh, gf, wr)


def _first_argmax(vals, n):
    top = jnp.max(vals, axis=0, keepdims=True)
    rows = lax.broadcasted_iota(jnp.int32, vals.shape, 0)
    idx = jnp.min(jnp.where(vals == top, rows, n), axis=0, keepdims=True)
    return top, idx, rows


def _route_kernel(lg_ref, bias_ref, ids_ref, c1_ref, c2_ref):
    lt = lg_ref[...].T + bias_ref[...]
    gl = lt[0:N_GROUPS]
    ge = jnp.exp(gl - jnp.max(gl, axis=0, keepdims=True))
    gp = ge / jnp.sum(ge, axis=0, keepdims=True)
    p_g, g_idx, _ = _first_argmax(gp, N_GROUPS)
    esel = lt[EXPERT_ROW0:EXPERT_ROW0 + EPG]
    for gg in range(1, N_GROUPS):
        esel = jnp.where(g_idx == gg, lt[EXPERT_ROW0 + EPG * gg:EXPERT_ROW0 + EPG * (gg + 1)], esel)
    ee = jnp.exp(esel - jnp.max(esel, axis=0, keepdims=True))
    pe = ee / jnp.sum(ee, axis=0, keepdims=True)
    p1, i1, rows = _first_argmax(pe, EPG)
    p2, i2, _ = _first_argmax(jnp.where(rows == i1, -1.0, pe), EPG)
    den = p1 + p2
    c1 = p_g * (p1 / den)
    c2 = p_g * (p2 / den)
    e1 = g_idx * EPG + i1
    e2 = g_idx * EPG + i2
    tm = lt.shape[1]
    ids_ref[...] = jnp.concatenate([e1, e2, jnp.zeros((6, tm), jnp.int32)], axis=0)
    c1_ref[...] = jnp.broadcast_to(c1, (LANES, tm)).T
    c2_ref[...] = jnp.broadcast_to(c2, (LANES, tm)).T


def _route(lg, bias, *, tm):
    T = lg.shape[0]
    tok = pl.BlockSpec((tm, LANES), lambda i: (i, 0))
    return pl.pallas_call(
        _route_kernel, grid=(T // tm,),
        in_specs=[pl.BlockSpec((tm, ROUTE_W), lambda i: (i, 0)), _const_spec((ROUTE_W, 1))],
        out_specs=[pl.BlockSpec((8, tm), lambda i: (0, i)), tok, tok],
        out_shape=[jax.ShapeDtypeStruct((8, T), jnp.int32), jax.ShapeDtypeStruct((T, LANES), F32),
                   jax.ShapeDtypeStruct((T, LANES), F32)],
        compiler_params=_params(("parallel",)), name="route",
    )(lg, bias)


def _routing_tables(ids, tmx):
    T = ids.shape[1]
    P = TOP_K * T
    ef = ids.reshape(P)
    onehot = (ef[:, None] == jnp.arange(N_EXPERTS, dtype=jnp.int32)[None, :]).astype(jnp.int32)
    csum = jnp.cumsum(onehot, axis=0)
    cnt = csum[-1]
    rank = jnp.take_along_axis(csum, ef[:, None], axis=1)[:, 0] - 1
    padded = ((cnt + tmx - 1) // tmx) * tmx
    off = jnp.cumsum(padded) - padded
    pos = (off[ef] + rank).astype(jnp.int32)
    n_tiles = (P + N_EXPERTS * tmx) // tmx
    ends = off + padded
    tile_start = jnp.arange(n_tiles, dtype=jnp.int32) * tmx
    tile_e = jnp.sum((tile_start[:, None] >= ends[None, :]).astype(jnp.int32), axis=1)
    tile_valid = (tile_e < N_EXPERTS).astype(jnp.int32)
    tile_e = jnp.minimum(tile_e, N_EXPERTS - 1)
    used = jnp.sum(padded)
    pad_start = jnp.concatenate([off + cnt, used[None]]).astype(jnp.int32)
    n_pad = jnp.concatenate([padded - cnt, (n_tiles * tmx - used)[None]]).astype(jnp.int32)
    return pos.reshape(TOP_K, T), tile_e, tile_valid, pad_start, n_pad, n_tiles


def _dispatch_kernel(pad_start_ref, n_pad_ref, p1_ref, p2_ref, h_ref, xs_hbm, zero_ref, sem, zsem):
    i = pl.program_id(0)
    td = p1_ref.shape[2]

    @pl.when(i == 0)
    def _():
        zero_ref[...] = jnp.zeros(zero_ref.shape, ROW_DT)

        def per_range(e, carry):
            first = pad_start_ref[e]

            def fill(r, c):
                pltpu.make_async_copy(zero_ref, _row_tile(xs_hbm, first + r), zsem).start()
                return c

            def drain(r, c):
                pltpu.make_async_copy(zero_ref, _row_tile(xs_hbm, first), zsem).wait()
                return c

            lax.fori_loop(0, n_pad_ref[e], fill, 0)
            lax.fori_loop(0, n_pad_ref[e], drain, 0)
            return carry

        lax.fori_loop(0, pad_start_ref.shape[0], per_range, 0)

    def issue(r, carry):
        src = _row_tile(h_ref, r)
        pltpu.make_async_copy(src, _row_tile(xs_hbm, p1_ref[0, 0, r]), sem).start(priority=0)
        pltpu.make_async_copy(src, _row_tile(xs_hbm, p2_ref[0, 0, r]), sem).start(priority=1)
        return carry

    lax.fori_loop(0, td, issue, 0, unroll=8)
    for _ in range(TOP_K):
        pltpu.make_async_copy(h_ref, xs_hbm.at[pl.ds(0, td * ROW_TILES), :], sem).wait()


def _dispatch(pos, h2, pad_start, n_pad, n_rows, *, td):
    T = pos.shape[1]
    n = T // td
    idx = pl.BlockSpec((1, 1, td), lambda i, ps, npd: (i, 0, 0), memory_space=pltpu.SMEM)
    grid_spec = pltpu.PrefetchScalarGridSpec(
        num_scalar_prefetch=2, grid=(n,),
        in_specs=[idx, idx, pl.BlockSpec((td * ROW_TILES, LANES), lambda i, ps, npd: (i, 0))],
        out_specs=pl.BlockSpec(memory_space=pl.ANY),
        scratch_shapes=[pltpu.VMEM((ROW_TILES, LANES), ROW_DT), pltpu.SemaphoreType.DMA(()),
                        pltpu.SemaphoreType.DMA(())])
    return pl.pallas_call(
        _dispatch_kernel, grid_spec=grid_spec,
        out_shape=jax.ShapeDtypeStruct((n_rows * ROW_TILES, LANES), ROW_DT),
        compiler_params=_params(("arbitrary",)), name="dispatch",
    )(pad_start, n_pad, pos[0].reshape(n, 1, td), pos[1].reshape(n, 1, td), h2)


def _start_row_gather(idx_ref, src_hbm, dst, sem, n_rows):
    def issue(p, carry):
        for prio in range(2):
            r = 2 * p + prio
            pltpu.make_async_copy(_row_tile(src_hbm, idx_ref[0, 0, r]), _row_tile(dst, r), sem).start(priority=prio)
        return carry

    lax.fori_loop(0, n_rows // 2, issue, 0, unroll=4)


def _wait_row_gather(src_hbm, dst, sem, n_rows):
    pltpu.make_async_copy(src_hbm.at[pl.ds(0, n_rows * ROW_TILES), :], dst, sem).wait()


def _moe_kernel(te_ref, tv_ref, x_ref, wg_ref, wu_ref, wd_ref, y_ref, wg_s, wu_s, wd_s):
    i = pl.program_id(0)

    @pl.when((i == 0) | (te_ref[i] != te_ref[jnp.maximum(i - 1, 0)]))
    def _():
        wg_s[...] = wg_ref[0].astype(BF16)
        wu_s[...] = wu_ref[0].astype(BF16)
        wd_s[...] = wd_ref[0].astype(BF16)

    @pl.when(tv_ref[i] == 1)
    def _():
        x = _load_token_tiles(x_ref).astype(BF16)
        hid = _silu(_dot(x, wg_s[...])) * _dot(x, wu_s[...])
        _store_token_tiles(y_ref, _dot(hid.astype(BF16), wd_s[...]))

    @pl.when(tv_ref[i] == 0)
    def _():
        y_ref[...] = jnp.zeros(y_ref.shape, ROW_DT)


def _moe(xs, tile_e, tile_valid, wg, wu, wd, *, tmx):
    n_tiles = tile_e.shape[0]
    rows = pl.BlockSpec((tmx * ROW_TILES, LANES), lambda i, te, tv: (i, 0))
    rows_in = pl.BlockSpec((tmx * ROW_TILES, LANES), lambda i, te, tv: (i * tv[i], 0))
    grid_spec = pltpu.PrefetchScalarGridSpec(
        num_scalar_prefetch=2, grid=(n_tiles,),
        in_specs=[rows_in,
                  pl.BlockSpec((1, D_MODEL, D_EXPERT), lambda i, te, tv: (te[i], 0, 0)),
                  pl.BlockSpec((1, D_MODEL, D_EXPERT), lambda i, te, tv: (te[i], 0, 0)),
                  pl.BlockSpec((1, D_EXPERT, D_MODEL), lambda i, te, tv: (te[i], 0, 0))],
        out_specs=rows,
        scratch_shapes=[pltpu.VMEM((D_MODEL, D_EXPERT), BF16), pltpu.VMEM((D_MODEL, D_EXPERT), BF16),
                        pltpu.VMEM((D_EXPERT, D_MODEL), BF16)])
    return pl.pallas_call(
        _moe_kernel, grid_spec=grid_spec,
        out_shape=jax.ShapeDtypeStruct((n_tiles * tmx * ROW_TILES, LANES), ROW_DT),
        compiler_params=_params(("arbitrary",)), name="moe",
    )(tile_e, tile_valid, xs, wg, wu, wd)


def _combine_kernel(p1c_ref, p2c_ref, p1n_ref, p2n_ref, x1_ref, c1_ref, c2_ref, y_hbm, o_ref, ybuf, sem):
    i = pl.program_id(0)
    n = pl.num_programs(0)
    tm = o_ref.shape[0]
    slot = i % 2

    def start(p1_ref, p2_ref, s):
        _start_row_gather(p1_ref, y_hbm, ybuf.at[s, 0], sem.at[s], tm)
        _start_row_gather(p2_ref, y_hbm, ybuf.at[s, 1], sem.at[s], tm)

    @pl.when(i == 0)
    def _():
        start(p1c_ref, p2c_ref, 0)

    @pl.when(i + 1 < n)
    def _():
        start(p1n_ref, p2n_ref, 1 - slot)

    _wait_row_gather(y_hbm, ybuf.at[slot, 0], sem.at[slot], tm)
    _wait_row_gather(y_hbm, ybuf.at[slot, 1], sem.at[slot], tm)
    reps = D_MODEL // LANES
    c1 = jnp.tile(c1_ref[...], (1, reps))
    c2 = jnp.tile(c2_ref[...], (1, reps))
    o_ref[...] = (x1_ref[...] + c1 * _load_token_tiles(ybuf.at[slot, 0])
                  + c2 * _load_token_tiles(ybuf.at[slot, 1]))


def _combine(pos, x1, c1, c2, y, *, tm):
    T = x1.shape[0]
    n = T // tm
    p1 = pos[0].reshape(n, 1, tm)
    p2 = pos[1].reshape(n, 1, tm)
    cur = pl.BlockSpec((1, 1, tm), lambda i: (i, 0, 0), memory_space=pltpu.SMEM)
    nxt = pl.BlockSpec((1, 1, tm), lambda i: (jnp.minimum(i + 1, n - 1), 0, 0), memory_space=pltpu.SMEM)
    tok = lambda w: pl.BlockSpec((tm, w), lambda i: (i, 0))
    return pl.pallas_call(
        _combine_kernel, grid=(n,),
        in_specs=[cur, cur, nxt, nxt, tok(D_MODEL), tok(LANES), tok(LANES), pl.BlockSpec(memory_space=pl.ANY)],
        out_specs=tok(D_MODEL),
        out_shape=jax.ShapeDtypeStruct((T, D_MODEL), F32),
        scratch_shapes=[pltpu.VMEM((2, TOP_K, tm * ROW_TILES, LANES), ROW_DT), pltpu.SemaphoreType.DMA((2,))],
        compiler_params=_params(("arbitrary",)), name="combine",
    )(p1, p2, p1, p2, x1, c1, c2, y)


def _tiles(S):
    return dict(tm=min(512, S), tq=min(512, S), cb=min(256, S), tmerge=min(512, S), troute=min(1024, S),
                tmoe=512, tcomb=256, tdisp=min(512, S))


def _rope_tables(S):
    rows = S // GRID_W
    row_ids = jnp.repeat(jnp.arange(rows), GRID_W).astype(F32)
    col_ids = jnp.tile(jnp.arange(GRID_W), rows).astype(F32)
    inv_freq = ROPE_THETA ** (-jnp.arange(0, HALF, 2, dtype=F32) / HALF)
    ang = jnp.concatenate([row_ids[:, None] * inv_freq, col_ids[:, None] * inv_freq], axis=-1)
    return jnp.cos(ang), jnp.sin(ang)


def kernel(x, g_mix, w_in, q_norm, k_norm, hgrn_norm, lb_fwd, lb_bwd, w_attn_branch, w_hgrn_branch, w_out, g_ffn,
           w_router_group, b_router_group, w_router_expert, b_router_expert, w_exp_gate, w_exp_up, w_exp_down):
    B, S, D = x.shape
    assert D == D_MODEL and w_in.shape == (1, D_MODEL, D_IN) and lb_fwd.shape[0] == 2
    T = B * S
    t = _tiles(S)

    perm = np.concatenate([np.arange(0, HEAD_DIM, 2), np.arange(1, HEAD_DIM, 2)])
    qcols = (np.arange(ATTN_HEADS)[:, None] * HEAD_DIM + perm[None, :]).reshape(-1)
    kcols = IN_OFF[1] + (np.arange(ATTN_KV_HEADS)[:, None] * HEAD_DIM + perm[None, :]).reshape(-1)
    cols = np.concatenate([qcols, kcols, np.arange(IN_OFF[2], D_IN)])
    w_all = w_in[0][:, cols].astype(BF16)
    gq = q_norm[0][perm][:, None].astype(F32)
    gk = k_norm[0][perm][:, None].astype(F32)
    cos, sin = _rope_tables(S)

    (qt, k, kn2, vt, hq, hv, kf, lff, kb, lfb, og, sa, sb) = _in_proj(
        x.reshape(T, D), g_mix.astype(F32), w_all, gq, gk, cos.T, sin.T,
        lb_fwd.astype(F32), lb_bwd.astype(F32), B=B, S=S, tm=t["tm"])

    ao = _attention(qt, k, kn2, vt, B=B, S=S, tq=t["tq"]).reshape(T, ATTN_Q_W)
    of, ob = _hgrn(hq, hv, kf, lff, kb, lfb, B=B, S=S, cb=t["cb"])

    wr = jnp.zeros((D_MODEL, ROUTE_W), F32)
    wr = wr.at[:, 0:N_GROUPS].set(w_router_group[0]).at[:, EXPERT_ROW0:EXPERT_ROW0 + N_EXPERTS].set(w_router_expert[0])
    rbias = jnp.zeros((ROUTE_W, 1), F32)
    rbias = rbias.at[0:N_GROUPS, 0].set(b_router_group[0]).at[EXPERT_ROW0:EXPERT_ROW0 + N_EXPERTS, 0].set(
        b_router_expert[0])
    x1, h2, lg = _merge(
        ao, of, ob, og, sa, sb, x.reshape(T, D), w_attn_branch[0].astype(BF16), w_hgrn_branch[0].astype(BF16),
        w_out[0].astype(BF16), jnp.tile(hgrn_norm[0], HGRN_HEADS)[None, :].astype(F32), g_ffn.astype(F32),
        wr.astype(BF16), tm=t["tmerge"])
    ids, c1, c2 = _route(lg, rbias, tm=t["troute"])
    pos, tile_e, tile_valid, pad_start, n_pad, n_tiles = _routing_tables(ids[0:TOP_K], t["tmoe"])
    xs = _dispatch(pos, h2, pad_start, n_pad, n_tiles * t["tmoe"], td=t["tdisp"])
    y = _moe(xs, tile_e, tile_valid, w_exp_gate[0], w_exp_up[0], w_exp_down[0], tmx=t["tmoe"])
    out = _combine(pos, x1, c1, c2, y, tm=t["tcomb"])
    return out.reshape(B, S, D)
```

```python
import math

import jax
import jax.numpy as jnp
import numpy as np
from jax import lax
from jax.experimental import pallas as pl
from jax.experimental.pallas import tpu as pltpu

F32 = jnp.float32
BF16 = jnp.bfloat16

D_MODEL = 1024
GRID_W = 64
EPS = 1e-6
ATTN_HEADS = 8
ATTN_KV_HEADS = 2
GQA_GROUP = ATTN_HEADS // ATTN_KV_HEADS
HEAD_DIM = 64
HALF = HEAD_DIM // 2
ROPE_THETA = 10000.0
HGRN_HEADS = 4
HGRN_DK = 128
HGRN_CHUNK = 32
HGRN_SCALE = HGRN_DK ** -0.5
N_GROUPS = 4
EPG = 8
N_EXPERTS = N_GROUPS * EPG
TOP_K = 2
D_EXPERT = 512
ATTN_Q_W = ATTN_HEADS * HEAD_DIM
ATTN_KV_W = ATTN_KV_HEADS * HEAD_DIM
HGRN_W = HGRN_HEADS * HGRN_DK
IN_SPLITS = (ATTN_Q_W, ATTN_KV_W, ATTN_KV_W, HGRN_W, HGRN_W, HGRN_W, HGRN_W, HGRN_W, D_MODEL, D_MODEL)
IN_OFF = tuple(int(v) for v in np.cumsum((0,) + IN_SPLITS))
D_IN = IN_OFF[-1]
LANES = 128
SUBLANES = 8
ROUTE_W = LANES
EXPERT_ROW0 = 8
VMEM_LIMIT = 56 * 1024 * 1024
NEG_BIG = -1e30
LOG2E = math.log2(math.e)
SAFE_SCORE_BOUND = 60.0


def _sigmoid(x):
    return 1.0 / (1.0 + jnp.exp(-x))


def _silu(x):
    return x * _sigmoid(x)


def _dot(a, b):
    return jnp.dot(a, b, preferred_element_type=F32)


def _params(sem):
    return pltpu.CompilerParams(dimension_semantics=sem, vmem_limit_bytes=VMEM_LIMIT)


def _const_spec(shape):
    nd = len(shape)
    return pl.BlockSpec(shape, lambda *_: (0,) * nd)


ROW_TILES = D_MODEL // LANES
ROW_DT = F32


def _store_token_tiles(ref, val):
    rows = val.shape[0]
    for j in range(ROW_TILES):
        ref[pl.ds(j, rows, stride=ROW_TILES), :] = val[:, LANES * j:LANES * (j + 1)]


def _load_token_tiles(ref):
    rows = ref.shape[0] // ROW_TILES
    return jnp.concatenate([ref[pl.ds(j, rows, stride=ROW_TILES), :] for j in range(ROW_TILES)], axis=1)


def _row_tile(ref, r):
    return ref.at[pl.ds(pl.multiple_of(r * ROW_TILES, ROW_TILES), ROW_TILES), :]


INPROJ_SUBTILES = 4

def _inproj_kernel(x_ref, g_ref, w_ref, gq_ref, gk_ref, cos_ref, sin_ref, lbf_ref, lbb_ref,
                   qt_ref, k_ref, kn2_ref, vt_ref, hq_ref, hv_ref, kf_ref, lff_ref, kb_ref, lfb_ref,
                   og_ref, sa_ref, sb_ref):
    sub = x_ref.shape[0] // INPROJ_SUBTILES
    rows = [slice(sub * s, sub * (s + 1)) for s in range(INPROJ_SUBTILES)]
    hs = []
    for r in rows:
        x = x_ref[r, :]
        ms = jnp.mean(x * x, axis=-1, keepdims=True)
        hs.append((x * lax.rsqrt(ms + EPS) * g_ref[...]).astype(BF16))

    def proj(i, s):
        return _dot(hs[s], w_ref[:, IN_OFF[i]:IN_OFF[i + 1]])

    def norm_rope_t(y, n_heads, gain_ref, r):
        yt = y.T.reshape(n_heads, HEAD_DIM, sub)
        ss = jnp.sum(yt * yt, axis=1, keepdims=True)
        yn = yt * lax.rsqrt(ss * (1.0 / HEAD_DIM) + EPS) * gain_ref[...][None]
        a, b = yn[:, :HALF, :], yn[:, HALF:, :]
        c, s = cos_ref[:, r][None], sin_ref[:, r][None]
        return jnp.concatenate([a * c - b * s, a * s + b * c], axis=1)

    def lower_bound(lbraw_ref):
        raw = lbraw_ref[...]
        e = jnp.exp(raw - jnp.max(raw, axis=0, keepdims=True))
        return e[0:1] / jnp.sum(e, axis=0, keepdims=True)

    def direction(z, lb, k_out, lf_out, r):
        k_out[r, :] = ((1.0 - lb) * _sigmoid(-z)).astype(BF16)
        lf_out[r, :] = jnp.log(lb + (1.0 - lb) * _sigmoid(z))

    for s, r in enumerate(rows):
        qt_ref[0, :, :, r] = (norm_rope_t(proj(0, s), ATTN_HEADS, gq_ref, r)
                              * (HEAD_DIM ** -0.5 * LOG2E)).astype(BF16)
    for s, r in enumerate(rows):
        kt = norm_rope_t(proj(1, s), ATTN_KV_HEADS, gk_ref, r).astype(BF16).astype(F32)
        kn2_ref[0, :, r] = jnp.sum(kt * kt, axis=1)
        k = kt.reshape(ATTN_KV_W, sub).T
        for gg in range(ATTN_KV_HEADS):
            k_ref[0, gg, r, :] = k[:, HEAD_DIM * gg:HEAD_DIM * (gg + 1)].astype(BF16)
    for s, r in enumerate(rows):
        vt_ref[0, :, 0, :, r] = proj(2, s).T.reshape(ATTN_KV_HEADS, HEAD_DIM, sub).astype(BF16)
    for s, r in enumerate(rows):
        hq_ref[r, :] = (_silu(proj(3, s)) * HGRN_SCALE).astype(BF16)
    lb_f, lb_b = lower_bound(lbf_ref), lower_bound(lbb_ref)
    for s, r in enumerate(rows):
        direction(proj(4, s), lb_f, kf_ref, lff_ref, r)
    for s, r in enumerate(rows):
        direction(proj(5, s), lb_b, kb_ref, lfb_ref, r)
    for s, r in enumerate(rows):
        hv_ref[r, :] = proj(6, s).astype(BF16)
    for s, r in enumerate(rows):
        og_ref[r, :] = _silu(proj(7, s)).astype(BF16)
    for s, r in enumerate(rows):
        sa_ref[r, :] = _sigmoid(proj(8, s)).astype(BF16)
    for s, r in enumerate(rows):
        sb_ref[r, :] = _sigmoid(proj(9, s)).astype(BF16)


def _in_proj(x2d, g_mix, w_all, gq, gk, cos_t, sin_t, lb_fwd, lb_bwd, *, B, S, tm):
    T = B * S
    nsb = S // tm
    tok = lambda w: pl.BlockSpec((tm, w), lambda i: (i, 0))
    in_specs = [
        tok(D_MODEL), _const_spec((1, D_MODEL)), _const_spec((D_MODEL, D_IN)),
        _const_spec((HEAD_DIM, 1)), _const_spec((HEAD_DIM, 1)),
        pl.BlockSpec((HALF, tm), lambda i: (0, i % nsb)), pl.BlockSpec((HALF, tm), lambda i: (0, i % nsb)),
        _const_spec(lb_fwd.shape), _const_spec(lb_bwd.shape),
    ]
    out_shape = [
        jax.ShapeDtypeStruct((B, ATTN_HEADS, HEAD_DIM, S), BF16),
        jax.ShapeDtypeStruct((B, ATTN_KV_HEADS, S, HEAD_DIM), BF16),
        jax.ShapeDtypeStruct((B, ATTN_KV_HEADS, S), F32),
        jax.ShapeDtypeStruct((B, ATTN_KV_HEADS, nsb, HEAD_DIM, tm), BF16),
        jax.ShapeDtypeStruct((T, HGRN_W), BF16),
        jax.ShapeDtypeStruct((T, HGRN_W), BF16),
        jax.ShapeDtypeStruct((T, HGRN_W), BF16),
        jax.ShapeDtypeStruct((T, HGRN_W), F32),
        jax.ShapeDtypeStruct((T, HGRN_W), BF16),
        jax.ShapeDtypeStruct((T, HGRN_W), F32),
        jax.ShapeDtypeStruct((T, HGRN_W), BF16),
        jax.ShapeDtypeStruct((T, D_MODEL), BF16),
        jax.ShapeDtypeStruct((T, D_MODEL), BF16),
    ]
    out_specs = [
        pl.BlockSpec((1, ATTN_HEADS, HEAD_DIM, tm), lambda i: (i // nsb, 0, 0, i % nsb)),
        pl.BlockSpec((1, ATTN_KV_HEADS, tm, HEAD_DIM), lambda i: (i // nsb, 0, i % nsb, 0)),
        pl.BlockSpec((1, ATTN_KV_HEADS, tm), lambda i: (i // nsb, 0, i % nsb)),
        pl.BlockSpec((1, ATTN_KV_HEADS, 1, HEAD_DIM, tm), lambda i: (i // nsb, 0, i % nsb, 0, 0)),
        tok(HGRN_W), tok(HGRN_W), tok(HGRN_W), tok(HGRN_W), tok(HGRN_W), tok(HGRN_W), tok(HGRN_W),
        tok(D_MODEL), tok(D_MODEL),
    ]
    return pl.pallas_call(
        _inproj_kernel, grid=(T // tm,), in_specs=in_specs, out_specs=out_specs, out_shape=out_shape,
        compiler_params=_params(("parallel",)), name="in_proj",
    )(x2d, g_mix, w_all, gq, gk, cos_t, sin_t, lb_fwd, lb_bwd)


def _attn_kernel(qt_ref, k_ref, kn2_ref, vt_ref, o_ref, shift_ref, l_ref, acc_ref):
    g = pl.program_id(1)
    nk, tk = vt_ref.shape[2], vt_ref.shape[4]

    def keys(j):
        return k_ref[0, 0, pl.ds(pl.multiple_of(j * tk, tk), tk), :]

    kmax2 = jnp.max(kn2_ref[0, pl.ds(g, 1), :], axis=1, keepdims=True)
    for hh in range(GQA_GROUP):
        q = qt_ref[0, hh].astype(F32)
        shift_ref[hh] = jnp.sqrt(jnp.sum(q * q, axis=0, keepdims=True) * kmax2)

    @pl.when(jnp.max(shift_ref[...]) > SAFE_SCORE_BOUND)
    def _():
        shift_ref[...] = jnp.full(shift_ref.shape, NEG_BIG, F32)

        def max_body(j, carry):
            k = keys(j)
            for hh in range(GQA_GROUP):
                st = _dot(k, qt_ref[0, hh])
                shift_ref[hh] = jnp.maximum(shift_ref[hh], jnp.max(st, axis=0, keepdims=True))
            return carry

        lax.fori_loop(0, nk, max_body, 0)

    acc_ref[...] = jnp.zeros(acc_ref.shape, F32)
    l_ref[...] = jnp.zeros(l_ref.shape, F32)
    tq = qt_ref.shape[3]

    def body(j, carry):
        k = keys(j)
        vt = vt_ref[0, 0, j]
        sts = [_dot(k, qt_ref[0, hh]) for hh in range(GQA_GROUP)]
        for hh in range(GQA_GROUP):
            pt = jnp.exp2(sts[hh] - shift_ref[hh])
            l_ref[hh] += jnp.sum(pt.reshape(tk // SUBLANES, SUBLANES, tq), axis=0)
            acc_ref[hh] += _dot(vt, pt.astype(BF16))
        return carry

    lax.fori_loop(0, nk, body, 0, unroll=8)
    outs = [acc_ref[hh] * (1.0 / jnp.sum(l_ref[hh], axis=0, keepdims=True)) for hh in range(GQA_GROUP)]
    o_ref[0] = jnp.concatenate(outs, axis=0).T.astype(BF16)


def _attention(qt, k, kn2, vt, *, B, S, tq):
    nk, tk = vt.shape[2], vt.shape[4]
    gw = GQA_GROUP * HEAD_DIM
    return pl.pallas_call(
        _attn_kernel, grid=(B, ATTN_KV_HEADS, S // tq),
        in_specs=[
            pl.BlockSpec((1, GQA_GROUP, HEAD_DIM, tq), lambda b, g, i: (b, g, 0, i)),
            pl.BlockSpec((1, 1, S, HEAD_DIM), lambda b, g, i: (b, g, 0, 0)),
            pl.BlockSpec((1, ATTN_KV_HEADS, S), lambda b, g, i: (b, 0, 0)),
            pl.BlockSpec((1, 1, nk, HEAD_DIM, tk), lambda b, g, i: (b, g, 0, 0, 0)),
        ],
        out_specs=pl.BlockSpec((1, tq, gw), lambda b, g, i: (b, i, g)),
        out_shape=jax.ShapeDtypeStruct((B, S, ATTN_Q_W), BF16),
        scratch_shapes=[pltpu.VMEM((GQA_GROUP, 1, tq), F32), pltpu.VMEM((GQA_GROUP, SUBLANES, tq), F32),
                        pltpu.VMEM((GQA_GROUP, HEAD_DIM, tq), F32)],
        compiler_params=_params(("parallel", "parallel", "parallel")), name="attn",
    )(qt, k, kn2, vt)


def _hgrn_run_matrix(cb, reverse):
    C = HGRN_CHUNK
    t = np.arange(cb)[:, None]
    s = np.arange(cb)[None, :]
    same = (t // C) == (s // C)
    return jnp.asarray(same & ((s >= t) if reverse else (s <= t)), dtype=BF16)


_TN = (((0,), (0,)), ((), ()))
_NT = (((1,), (1,)), ((), ()))


def _hgrn_kernel(qf_ref, vf_ref, kf_ref, lff_ref, qb_ref, vb_ref, kb_ref, lfb_ref, runf_ref, runb_ref,
                 of_ref, ob_ref, stf_ref, stb_ref):
    @pl.when(pl.program_id(1) == 0)
    def _():
        stf_ref[...] = jnp.zeros(stf_ref.shape, F32)
        stb_ref[...] = jnp.zeros(stb_ref.shape, F32)

    cb = qf_ref.shape[0]
    C = HGRN_CHUNK
    n = cb // C
    row = lax.broadcasted_iota(jnp.int32, (cb, cb), 0)
    col = lax.broadcasted_iota(jnp.int32, (cb, cb), 1)
    same = (row // C) == (col // C)
    dirs = [
        dict(q=qf_ref, v=vf_ref, k=kf_ref, lf=lff_ref, run=runf_ref, o=of_ref, st=stf_ref, rev=False,
             mask=same & (col <= row), last=C - 1, mid=C // 2),
        dict(q=qb_ref, v=vb_ref, k=kb_ref, lf=lfb_ref, run=runb_ref, o=ob_ref, st=stb_ref, rev=True,
             mask=same & (col >= row), last=0, mid=C - 1 - C // 2),
    ]
    heads = [slice(HGRN_DK * hh, HGRN_DK * (hh + 1)) for hh in range(HGRN_HEADS)]

    for d in dirs:
        lf = d["lf"][...]
        lf_hi = lf.astype(BF16)
        lf_lo = (lf - lf_hi.astype(F32)).astype(BF16)
        d["b"] = _dot(d["run"][...], lf_hi) + _dot(d["run"][...], lf_lo)
    for d in dirs:
        b = d["b"]
        b3 = b.reshape(n, C, HGRN_W)
        b_last = jnp.broadcast_to(b3[:, d["last"]:d["last"] + 1, :], b3.shape).reshape(cb, HGRN_W)
        b_ref = jnp.broadcast_to(b3[:, d["mid"]:d["mid"] + 1, :], b3.shape).reshape(cb, HGRN_W)
        q = d["q"][...].astype(F32)
        k = d["k"][...].astype(F32)
        d["qs"] = (q * jnp.exp(b - b_ref)).astype(BF16)
        d["ks"] = (k * jnp.exp(b_ref - b)).astype(BF16)
        d["qi"] = (q * jnp.exp(b)).astype(BF16)
        d["kst"] = (k * jnp.exp(b_last - b)).astype(BF16)
        d["dec"] = jnp.exp(b_last)
        d["vv"] = d["v"][...]
    for d in dirs:
        d["a"] = [jnp.where(d["mask"], lax.dot_general(d["qs"][:, sl], d["ks"][:, sl], _NT,
                                                       preferred_element_type=F32), 0.0).astype(BF16)
                  for sl in heads]
    for d in dirs:
        d["oi"] = [_dot(d["a"][hh], d["vv"][:, sl]) for hh, sl in enumerate(heads)]
    for d in dirs:
        d["upd"] = [[lax.dot_general(d["vv"][C * ci:C * (ci + 1), sl], d["kst"][C * ci:C * (ci + 1), sl], _TN,
                                     preferred_element_type=F32) for ci in range(n)] for sl in heads]
    for d in dirs:
        order = range(n - 1, -1, -1) if d["rev"] else range(n)
        d["states"] = []
        for hh, sl in enumerate(heads):
            st = d["st"][hh]
            seen = [None] * n
            for ci in order:
                seen[ci] = st.astype(BF16)
                st = d["dec"][C * ci:C * ci + 1, sl] * st + d["upd"][hh][ci]
            d["st"][hh] = st
            d["states"].append(seen)
    for d in dirs:
        for hh, sl in enumerate(heads):
            inter = [lax.dot_general(d["qi"][C * ci:C * (ci + 1), sl], d["states"][hh][ci], _NT,
                                     preferred_element_type=F32) for ci in range(n)]
            d["o"][:, sl] = (d["oi"][hh] + jnp.concatenate(inter, axis=0)).astype(BF16)


def _hgrn(hq, hv, kf, lff, kb, lfb, *, B, S, cb):
    T = B * S
    nblk = S // cb
    fwd = pl.BlockSpec((cb, HGRN_W), lambda b, i: (b * nblk + i, 0))
    bwd = pl.BlockSpec((cb, HGRN_W), lambda b, i: (b * nblk + nblk - 1 - i, 0))
    run = _const_spec((cb, cb))
    st = pltpu.VMEM((HGRN_HEADS, HGRN_DK, HGRN_DK), F32)
    return pl.pallas_call(
        _hgrn_kernel, grid=(B, nblk),
        in_specs=[fwd, fwd, fwd, fwd, bwd, bwd, bwd, bwd, run, run],
        out_specs=[fwd, bwd],
        out_shape=[jax.ShapeDtypeStruct((T, HGRN_W), BF16)] * 2,
        scratch_shapes=[st, st],
        compiler_params=_params(("parallel", "arbitrary")), name="hgrn",
    )(hq, hv, kf, lff, hq, hv, kb, lfb, _hgrn_run_matrix(cb, False), _hgrn_run_matrix(cb, True))


MERGE_SUBTILES = 2

def _merge_kernel(ao_ref, of_ref, ob_ref, og_ref, sa_ref, sb_ref, x_ref, wa_ref, wb_ref, wo_ref,
                  gh_ref, gf_ref, wr_ref, x1_ref, h2_ref, lg_ref):
    tm = x_ref.shape[0]
    sub = tm // MERGE_SUBTILES
    rows = [slice(sub * s, sub * (s + 1)) for s in range(MERGE_SUBTILES)]
    ya = [_dot(ao_ref[r, :], wa_ref[...]) for r in rows]
    hn = []
    for r in rows:
        o = of_ref[r, :].astype(F32) + ob_ref[r, :].astype(F32)
        parts = []
        for hh in range(HGRN_HEADS):
            oh = o[:, HGRN_DK * hh:HGRN_DK * (hh + 1)]
            parts.append(oh * lax.rsqrt(jnp.mean(oh * oh, axis=1, keepdims=True) + EPS))
        hn.append((jnp.concatenate(parts, axis=1) * gh_ref[...] * og_ref[r, :].astype(F32)).astype(BF16))
    yb = [_dot(h, wb_ref[...]) for h in hn]
    merged = [(sa_ref[r, :].astype(F32) * a + sb_ref[r, :].astype(F32) * b).astype(BF16)
              for r, a, b in zip(rows, ya, yb)]
    x1 = [x_ref[r, :] + _dot(m, wo_ref[...]) for r, m in zip(rows, merged)]
    h2 = [v * lax.rsqrt(jnp.mean(v * v, axis=1, keepdims=True) + EPS) * gf_ref[...] for v in x1]
    hi = [v.astype(BF16) for v in h2]
    lo = [(v - h.astype(F32)).astype(BF16) for v, h in zip(h2, hi)]
    lg = [_dot(h, wr_ref[...]) + _dot(l, wr_ref[...]) for h, l in zip(hi, lo)]
    for s, r in enumerate(rows):
        x1_ref[r, :] = x1[s]
        lg_ref[r, :] = lg[s]
        _store_token_tiles(h2_ref.at[pl.ds(sub * ROW_TILES * s, sub * ROW_TILES), :], h2[s])


def _merge(ao, of, ob, og, sa, sb, x2d, wa, wb, wo, gh, gf, wr, *, tm):
    T = x2d.shape[0]
    tok = lambda w: pl.BlockSpec((tm, w), lambda i: (i, 0))
    return pl.pallas_call(
        _merge_kernel, grid=(T // tm,),
        in_specs=[tok(ATTN_Q_W), tok(HGRN_W), tok(HGRN_W), tok(HGRN_W), tok(D_MODEL), tok(D_MODEL), tok(D_MODEL),
                  _const_spec(wa.shape), _const_spec(wb.shape), _const_spec(wo.shape),
                  _const_spec(gh.shape), _const_spec(gf.shape), _const_spec(wr.shape)],
        out_specs=[tok(D_MODEL), pl.BlockSpec((tm * ROW_TILES, LANES), lambda i: (i, 0)), tok(ROUTE_W)],
        out_shape=[jax.ShapeDtypeStruct((T, D_MODEL), F32), jax.ShapeDtypeStruct((T * ROW_TILES, LANES), ROW_DT),
                   jax.ShapeDtypeStruct((T, ROUTE_W), F32)],
        compiler_params=_params(("parallel",)), name="merge",
    )(ao, of, ob, og, sa, sb, x2d, wa, wb, wo, gh, gf, wr)


def _first_argmax(vals, n):
    top = jnp.max(vals, axis=0, keepdims=True)
    rows = lax.broadcasted_iota(jnp.int32, vals.shape, 0)
    idx = jnp.min(jnp.where(vals == top, rows, n), axis=0, keepdims=True)
    return top, idx, rows


def _route_kernel(lg_ref, bias_ref, ids_ref, c1_ref, c2_ref):
    lt = lg_ref[...].T + bias_ref[...]
    gl = lt[0:N_GROUPS]
    ge = jnp.exp(gl - jnp.max(gl, axis=0, keepdims=True))
    gp = ge / jnp.sum(ge, axis=0, keepdims=True)
    p_g, g_idx, _ = _first_argmax(gp, N_GROUPS)
    esel = lt[EXPERT_ROW0:EXPERT_ROW0 + EPG]
    for gg in range(1, N_GROUPS):
        esel = jnp.where(g_idx == gg, lt[EXPERT_ROW0 + EPG * gg:EXPERT_ROW0 + EPG * (gg + 1)], esel)
    ee = jnp.exp(esel - jnp.max(esel, axis=0, keepdims=True))
    pe = ee / jnp.sum(ee, axis=0, keepdims=True)
    p1, i1, rows = _first_argmax(pe, EPG)
    p2, i2, _ = _first_argmax(jnp.where(rows == i1, -1.0, pe), EPG)
    den = p1 + p2
    c1 = p_g * (p1 / den)
    c2 = p_g * (p2 / den)
    e1 = g_idx * EPG + i1
    e2 = g_idx * EPG + i2
    tm = lt.shape[1]
    ids_ref[...] = jnp.concatenate([e1, e2, jnp.zeros((SUBLANES - TOP_K, tm), jnp.int32)], axis=0)
    c1_ref[...] = jnp.broadcast_to(c1, (LANES, tm)).T
    c2_ref[...] = jnp.broadcast_to(c2, (LANES, tm)).T


def _route(lg, bias, *, tm):
    T = lg.shape[0]
    tok = pl.BlockSpec((tm, LANES), lambda i: (i, 0))
    return pl.pallas_call(
        _route_kernel, grid=(T // tm,),
        in_specs=[pl.BlockSpec((tm, ROUTE_W), lambda i: (i, 0)), _const_spec((ROUTE_W, 1))],
        out_specs=[pl.BlockSpec((SUBLANES, tm), lambda i: (0, i)), tok, tok],
        out_shape=[jax.ShapeDtypeStruct((SUBLANES, T), jnp.int32), jax.ShapeDtypeStruct((T, LANES), F32),
                   jax.ShapeDtypeStruct((T, LANES), F32)],
        compiler_params=_params(("parallel",)), name="route",
    )(lg, bias)


def _routing_tables(ids, tmx):
    T = ids.shape[1]
    P = TOP_K * T
    ef = ids.reshape(P)
    onehot = (ef[:, None] == jnp.arange(N_EXPERTS, dtype=jnp.int32)[None, :]).astype(jnp.int32)
    csum = jnp.cumsum(onehot, axis=0)
    cnt = csum[-1]
    rank = jnp.take_along_axis(csum, ef[:, None], axis=1)[:, 0] - 1
    padded = ((cnt + tmx - 1) // tmx) * tmx
    off = jnp.cumsum(padded) - padded
    pos = (off[ef] + rank).astype(jnp.int32)
    n_tiles = (P + N_EXPERTS * tmx) // tmx
    ends = off + padded
    tile_start = jnp.arange(n_tiles, dtype=jnp.int32) * tmx
    tile_e = jnp.sum((tile_start[:, None] >= ends[None, :]).astype(jnp.int32), axis=1)
    tile_valid = (tile_e < N_EXPERTS).astype(jnp.int32)
    tile_e = jnp.minimum(tile_e, N_EXPERTS - 1)
    used = jnp.sum(padded)
    pad_start = jnp.concatenate([off + cnt, used[None]]).astype(jnp.int32)
    n_pad = jnp.concatenate([padded - cnt, (n_tiles * tmx - used)[None]]).astype(jnp.int32)
    return pos.reshape(TOP_K, T), tile_e, tile_valid, pad_start, n_pad, n_tiles


def _dispatch_kernel(pad_start_ref, n_pad_ref, p1_ref, p2_ref, h_ref, xs_hbm, zero_ref, sem, zsem):
    i = pl.program_id(0)
    td = p1_ref.shape[2]

    @pl.when(i == 0)
    def _():
        zero_ref[...] = jnp.zeros(zero_ref.shape, ROW_DT)

        def per_range(e, carry):
            first = pad_start_ref[e]

            def fill(r, c):
                pltpu.make_async_copy(zero_ref, _row_tile(xs_hbm, first + r), zsem).start()
                return c

            def drain(r, c):
                pltpu.make_async_copy(zero_ref, _row_tile(xs_hbm, first), zsem).wait()
                return c

            lax.fori_loop(0, n_pad_ref[e], fill, 0)
            lax.fori_loop(0, n_pad_ref[e], drain, 0)
            return carry

        lax.fori_loop(0, pad_start_ref.shape[0], per_range, 0)

    def issue(r, carry):
        src = _row_tile(h_ref, r)
        pltpu.make_async_copy(src, _row_tile(xs_hbm, p1_ref[0, 0, r]), sem).start(priority=0)
        pltpu.make_async_copy(src, _row_tile(xs_hbm, p2_ref[0, 0, r]), sem).start(priority=1)
        return carry

    lax.fori_loop(0, td, issue, 0, unroll=8)
    for _ in range(TOP_K):
        pltpu.make_async_copy(h_ref, xs_hbm.at[pl.ds(0, td * ROW_TILES), :], sem).wait()


def _dispatch(pos, h2, pad_start, n_pad, n_rows, *, td):
    T = pos.shape[1]
    n = T // td
    idx = pl.BlockSpec((1, 1, td), lambda i, ps, npd: (i, 0, 0), memory_space=pltpu.SMEM)
    grid_spec = pltpu.PrefetchScalarGridSpec(
        num_scalar_prefetch=2, grid=(n,),
        in_specs=[idx, idx, pl.BlockSpec((td * ROW_TILES, LANES), lambda i, ps, npd: (i, 0))],
        out_specs=pl.BlockSpec(memory_space=pl.ANY),
        scratch_shapes=[pltpu.VMEM((ROW_TILES, LANES), ROW_DT), pltpu.SemaphoreType.DMA(()),
                        pltpu.SemaphoreType.DMA(())])
    return pl.pallas_call(
        _dispatch_kernel, grid_spec=grid_spec,
        out_shape=jax.ShapeDtypeStruct((n_rows * ROW_TILES, LANES), ROW_DT),
        compiler_params=_params(("arbitrary",)), name="dispatch",
    )(pad_start, n_pad, pos[0].reshape(n, 1, td), pos[1].reshape(n, 1, td), h2)


def _start_row_gather(idx_ref, src_hbm, dst, sem, n_rows):
    def issue(p, carry):
        for prio in range(2):
            r = 2 * p + prio
            pltpu.make_async_copy(_row_tile(src_hbm, idx_ref[0, 0, r]), _row_tile(dst, r), sem).start(priority=prio)
        return carry

    lax.fori_loop(0, n_rows // 2, issue, 0, unroll=4)


def _wait_row_gather(src_hbm, dst, sem, n_rows):
    pltpu.make_async_copy(src_hbm.at[pl.ds(0, n_rows * ROW_TILES), :], dst, sem).wait()


def _moe_kernel(te_ref, tv_ref, x_ref, wg_ref, wu_ref, wd_ref, y_ref, wg_s, wu_s, wd_s):
    i = pl.program_id(0)

    @pl.when((i == 0) | (te_ref[i] != te_ref[jnp.maximum(i - 1, 0)]))
    def _():
        wg_s[...] = wg_ref[0].astype(BF16)
        wu_s[...] = wu_ref[0].astype(BF16)
        wd_s[...] = wd_ref[0].astype(BF16)

    @pl.when(tv_ref[i] == 1)
    def _():
        x = _load_token_tiles(x_ref).astype(BF16)
        hid = _silu(_dot(x, wg_s[...])) * _dot(x, wu_s[...])
        _store_token_tiles(y_ref, _dot(hid.astype(BF16), wd_s[...]))

    @pl.when(tv_ref[i] == 0)
    def _():
        y_ref[...] = jnp.zeros(y_ref.shape, ROW_DT)


def _moe(xs, tile_e, tile_valid, wg, wu, wd, *, tmx):
    n_tiles = tile_e.shape[0]
    rows = pl.BlockSpec((tmx * ROW_TILES, LANES), lambda i, te, tv: (i, 0))
    rows_in = pl.BlockSpec((tmx * ROW_TILES, LANES), lambda i, te, tv: (i * tv[i], 0))
    grid_spec = pltpu.PrefetchScalarGridSpec(
        num_scalar_prefetch=2, grid=(n_tiles,),
        in_specs=[rows_in,
                  pl.BlockSpec((1, D_MODEL, D_EXPERT), lambda i, te, tv: (te[i], 0, 0)),
                  pl.BlockSpec((1, D_MODEL, D_EXPERT), lambda i, te, tv: (te[i], 0, 0)),
                  pl.BlockSpec((1, D_EXPERT, D_MODEL), lambda i, te, tv: (te[i], 0, 0))],
        out_specs=rows,
        scratch_shapes=[pltpu.VMEM((D_MODEL, D_EXPERT), BF16), pltpu.VMEM((D_MODEL, D_EXPERT), BF16),
                        pltpu.VMEM((D_EXPERT, D_MODEL), BF16)])
    return pl.pallas_call(
        _moe_kernel, grid_spec=grid_spec,
        out_shape=jax.ShapeDtypeStruct((n_tiles * tmx * ROW_TILES, LANES), ROW_DT),
        compiler_params=_params(("arbitrary",)), name="moe",
    )(tile_e, tile_valid, xs, wg, wu, wd)


def _combine_kernel(p1c_ref, p2c_ref, p1n_ref, p2n_ref, x1_ref, c1_ref, c2_ref, y_hbm, o_ref, ybuf, sem):
    i = pl.program_id(0)
    n = pl.num_programs(0)
    tm = o_ref.shape[0]
    slot = i % 2

    def start(p1_ref, p2_ref, s):
        _start_row_gather(p1_ref, y_hbm, ybuf.at[s, 0], sem.at[s], tm)
        _start_row_gather(p2_ref, y_hbm, ybuf.at[s, 1], sem.at[s], tm)

    @pl.when(i == 0)
    def _():
        start(p1c_ref, p2c_ref, 0)

    @pl.when(i + 1 < n)
    def _():
        start(p1n_ref, p2n_ref, 1 - slot)

    _wait_row_gather(y_hbm, ybuf.at[slot, 0], sem.at[slot], tm)
    _wait_row_gather(y_hbm, ybuf.at[slot, 1], sem.at[slot], tm)
    reps = D_MODEL // LANES
    c1 = jnp.tile(c1_ref[...], (1, reps))
    c2 = jnp.tile(c2_ref[...], (1, reps))
    o_ref[...] = (x1_ref[...] + c1 * _load_token_tiles(ybuf.at[slot, 0])
                  + c2 * _load_token_tiles(ybuf.at[slot, 1]))


def _combine(pos, x1, c1, c2, y, *, tm):
    T = x1.shape[0]
    n = T // tm
    p1 = pos[0].reshape(n, 1, tm)
    p2 = pos[1].reshape(n, 1, tm)
    cur = pl.BlockSpec((1, 1, tm), lambda i: (i, 0, 0), memory_space=pltpu.SMEM)
    nxt = pl.BlockSpec((1, 1, tm), lambda i: (jnp.minimum(i + 1, n - 1), 0, 0), memory_space=pltpu.SMEM)
    tok = lambda w: pl.BlockSpec((tm, w), lambda i: (i, 0))
    return pl.pallas_call(
        _combine_kernel, grid=(n,),
        in_specs=[cur, cur, nxt, nxt, tok(D_MODEL), tok(LANES), tok(LANES), pl.BlockSpec(memory_space=pl.ANY)],
        out_specs=tok(D_MODEL),
        out_shape=jax.ShapeDtypeStruct((T, D_MODEL), F32),
        scratch_shapes=[pltpu.VMEM((2, TOP_K, tm * ROW_TILES, LANES), ROW_DT), pltpu.SemaphoreType.DMA((2,))],
        compiler_params=_params(("arbitrary",)), name="combine",
    )(p1, p2, p1, p2, x1, c1, c2, y)


def _tiles(S):
    return dict(tm=min(512, S), tq=min(512, S), cb=min(256, S), tmerge=min(512, S), troute=min(1024, S),
                tmoe=512, tcomb=256, tdisp=min(512, S))


def _rope_tables(S):
    rows = S // GRID_W
    row_ids = jnp.repeat(jnp.arange(rows), GRID_W).astype(F32)
    col_ids = jnp.tile(jnp.arange(GRID_W), rows).astype(F32)
    inv_freq = ROPE_THETA ** (-jnp.arange(0, HALF, 2, dtype=F32) / HALF)
    ang = jnp.concatenate([row_ids[:, None] * inv_freq, col_ids[:, None] * inv_freq], axis=-1)
    return jnp.cos(ang), jnp.sin(ang)


def kernel(x, g_mix, w_in, q_norm, k_norm, hgrn_norm, lb_fwd, lb_bwd, w_attn_branch, w_hgrn_branch, w_out, g_ffn,
           w_router_group, b_router_group, w_router_expert, b_router_expert, w_exp_gate, w_exp_up, w_exp_down):
    B, S, D = x.shape
    assert D == D_MODEL and w_in.shape == (1, D_MODEL, D_IN) and lb_fwd.shape[0] == 2
    T = B * S
    t = _tiles(S)

    perm = np.concatenate([np.arange(0, HEAD_DIM, 2), np.arange(1, HEAD_DIM, 2)])
    qcols = (np.arange(ATTN_HEADS)[:, None] * HEAD_DIM + perm[None, :]).reshape(-1)
    kcols = IN_OFF[1] + (np.arange(ATTN_KV_HEADS)[:, None] * HEAD_DIM + perm[None, :]).reshape(-1)
    cols = np.concatenate([qcols, kcols, np.arange(IN_OFF[2], D_IN)])
    w_all = w_in[0][:, cols].astype(BF16)
    gq = q_norm[0][perm][:, None].astype(F32)
    gk = k_norm[0][perm][:, None].astype(F32)
    cos, sin = _rope_tables(S)

    (qt, k, kn2, vt, hq, hv, kf, lff, kb, lfb, og, sa, sb) = _in_proj(
        x.reshape(T, D), g_mix.astype(F32), w_all, gq, gk, cos.T, sin.T,
        lb_fwd.astype(F32), lb_bwd.astype(F32), B=B, S=S, tm=t["tm"])

    ao = _attention(qt, k, kn2, vt, B=B, S=S, tq=t["tq"]).reshape(T, ATTN_Q_W)
    of, ob = _hgrn(hq, hv, kf, lff, kb, lfb, B=B, S=S, cb=t["cb"])

    wr = jnp.zeros((D_MODEL, ROUTE_W), F32)
    wr = wr.at[:, 0:N_GROUPS].set(w_router_group[0]).at[:, EXPERT_ROW0:EXPERT_ROW0 + N_EXPERTS].set(w_router_expert[0])
    rbias = jnp.zeros((ROUTE_W, 1), F32)
    rbias = rbias.at[0:N_GROUPS, 0].set(b_router_group[0]).at[EXPERT_ROW0:EXPERT_ROW0 + N_EXPERTS, 0].set(
        b_router_expert[0])
    x1, h2, lg = _merge(
        ao, of, ob, og, sa, sb, x.reshape(T, D), w_attn_branch[0].astype(BF16), w_hgrn_branch[0].astype(BF16),
        w_out[0].astype(BF16), jnp.tile(hgrn_norm[0], HGRN_HEADS)[None, :].astype(F32), g_ffn.astype(F32),
        wr.astype(BF16), tm=t["tmerge"])
    ids, c1, c2 = _route(lg, rbias, tm=t["troute"])
    pos, tile_e, tile_valid, pad_start, n_pad, n_tiles = _routing_tables(ids[0:TOP_K], t["tmoe"])
    xs = _dispatch(pos, h2, pad_start, n_pad, n_tiles * t["tmoe"], td=t["tdisp"])
    y = _moe(xs, tile_e, tile_valid, w_exp_gate[0], w_exp_up[0], w_exp_down[0], tmx=t["tmoe"])
    out = _combine(pos, x1, c1, c2, y, tm=t["tcomb"])
    return out.reshape(B, S, D)
```

```python
import math

import jax
import jax.numpy as jnp
import numpy as np
from jax import lax
from jax.experimental import pallas as pl
from jax.experimental.pallas import tpu as pltpu

F32 = jnp.float32
BF16 = jnp.bfloat16

D_MODEL = 1024
GRID_W = 64
EPS = 1e-6
ATTN_HEADS = 8
ATTN_KV_HEADS = 2
GQA_GROUP = ATTN_HEADS // ATTN_KV_HEADS
HEAD_DIM = 64
HALF = HEAD_DIM // 2
ROPE_THETA = 10000.0
HGRN_HEADS = 4
HGRN_DK = 128
HGRN_CHUNK = 32
HGRN_SCALE = HGRN_DK ** -0.5
N_GROUPS = 4
EPG = 8
N_EXPERTS = N_GROUPS * EPG
TOP_K = 2
D_EXPERT = 512
ATTN_Q_W = ATTN_HEADS * HEAD_DIM
ATTN_KV_W = ATTN_KV_HEADS * HEAD_DIM
HGRN_W = HGRN_HEADS * HGRN_DK
IN_SPLITS = (ATTN_Q_W, ATTN_KV_W, ATTN_KV_W, HGRN_W, HGRN_W, HGRN_W, HGRN_W, HGRN_W, D_MODEL, D_MODEL)
IN_OFF = tuple(int(v) for v in np.cumsum((0,) + IN_SPLITS))
D_IN = IN_OFF[-1]
LANES = 128
SUBLANES = 8
ROUTE_W = LANES
EXPERT_ROW0 = 8
VMEM_LIMIT = 56 * 1024 * 1024
NEG_BIG = -1e30
LOG2E = math.log2(math.e)
SAFE_SCORE_BOUND = 60.0


def _sigmoid(x):
    return 1.0 / (1.0 + jnp.exp(-x))


def _silu(x):
    return x * _sigmoid(x)


def _dot(a, b):
    return jnp.dot(a, b, preferred_element_type=F32)


def _params(sem):
    return pltpu.CompilerParams(dimension_semantics=sem, vmem_limit_bytes=VMEM_LIMIT)


def _const_spec(shape):
    nd = len(shape)
    return pl.BlockSpec(shape, lambda *_: (0,) * nd)


ROW_TILES = D_MODEL // LANES
ROW_DT = F32


def _store_token_tiles(ref, val):
    rows = val.shape[0]
    for j in range(ROW_TILES):
        ref[pl.ds(j, rows, stride=ROW_TILES), :] = val[:, LANES * j:LANES * (j + 1)]


def _load_token_tiles(ref):
    rows = ref.shape[0] // ROW_TILES
    return jnp.concatenate([ref[pl.ds(j, rows, stride=ROW_TILES), :] for j in range(ROW_TILES)], axis=1)


def _row_tile(ref, r):
    return ref.at[pl.ds(pl.multiple_of(r * ROW_TILES, ROW_TILES), ROW_TILES), :]


INPROJ_SUBTILES = 4

def _inproj_kernel(x_ref, g_ref, w_ref, gq_ref, gk_ref, cos_ref, sin_ref, lbf_ref, lbb_ref,
                   qt_ref, k_ref, kn2_ref, vt_ref, hq_ref, hv_ref, kf_ref, lff_ref, kb_ref, lfb_ref,
                   og_ref, sa_ref, sb_ref):
    sub = x_ref.shape[0] // INPROJ_SUBTILES
    rows = [slice(sub * s, sub * (s + 1)) for s in range(INPROJ_SUBTILES)]
    hs = []
    for r in rows:
        x = x_ref[r, :]
        ms = jnp.mean(x * x, axis=-1, keepdims=True)
        hs.append((x * lax.rsqrt(ms + EPS) * g_ref[...]).astype(BF16))

    def proj(i, s):
        return _dot(hs[s], w_ref[:, IN_OFF[i]:IN_OFF[i + 1]])

    def norm_rope_t(y, n_heads, gain_ref, r):
        yt = y.T.reshape(n_heads, HEAD_DIM, sub)
        ss = jnp.sum(yt * yt, axis=1, keepdims=True)
        yn = yt * lax.rsqrt(ss * (1.0 / HEAD_DIM) + EPS) * gain_ref[...][None]
        a, b = yn[:, :HALF, :], yn[:, HALF:, :]
        c, s = cos_ref[:, r][None], sin_ref[:, r][None]
        return jnp.concatenate([a * c - b * s, a * s + b * c], axis=1)

    def lower_bound(lbraw_ref):
        raw = lbraw_ref[...]
        e = jnp.exp(raw - jnp.max(raw, axis=0, keepdims=True))
        return e[0:1] / jnp.sum(e, axis=0, keepdims=True)

    def direction(z, lb, k_out, lf_out, r):
        k_out[r, :] = ((1.0 - lb) * _sigmoid(-z)).astype(BF16)
        lf_out[r, :] = jnp.log(lb + (1.0 - lb) * _sigmoid(z))

    for s, r in enumerate(rows):
        qt_ref[0, :, :, r] = (norm_rope_t(proj(0, s), ATTN_HEADS, gq_ref, r)
                              * (HEAD_DIM ** -0.5 * LOG2E)).astype(BF16)
    for s, r in enumerate(rows):
        kt = norm_rope_t(proj(1, s), ATTN_KV_HEADS, gk_ref, r).astype(BF16).astype(F32)
        kn2_ref[0, :, r] = jnp.sum(kt * kt, axis=1)
        k = kt.reshape(ATTN_KV_W, sub).T
        for gg in range(ATTN_KV_HEADS):
            k_ref[0, gg, r, :] = k[:, HEAD_DIM * gg:HEAD_DIM * (gg + 1)].astype(BF16)
    for s, r in enumerate(rows):
        vt_ref[0, :, 0, :, r] = proj(2, s).T.reshape(ATTN_KV_HEADS, HEAD_DIM, sub).astype(BF16)
    for s, r in enumerate(rows):
        hq_ref[r, :] = (_silu(proj(3, s)) * HGRN_SCALE).astype(BF16)
    lb_f, lb_b = lower_bound(lbf_ref), lower_bound(lbb_ref)
    for s, r in enumerate(rows):
        direction(proj(4, s), lb_f, kf_ref, lff_ref, r)
    for s, r in enumerate(rows):
        direction(proj(5, s), lb_b, kb_ref, lfb_ref, r)
    for s, r in enumerate(rows):
        hv_ref[r, :] = proj(6, s).astype(BF16)
    for s, r in enumerate(rows):
        og_ref[r, :] = _silu(proj(7, s)).astype(BF16)
    for s, r in enumerate(rows):
        sa_ref[r, :] = _sigmoid(proj(8, s)).astype(BF16)
    for s, r in enumerate(rows):
        sb_ref[r, :] = _sigmoid(proj(9, s)).astype(BF16)


def _in_proj(x2d, g_mix, w_all, gq, gk, cos_t, sin_t, lb_fwd, lb_bwd, *, B, S, tm):
    T = B * S
    nsb = S // tm
    tok = lambda w: pl.BlockSpec((tm, w), lambda i: (i, 0))
    in_specs = [
        tok(D_MODEL), _const_spec((1, D_MODEL)), _const_spec((D_MODEL, D_IN)),
        _const_spec((HEAD_DIM, 1)), _const_spec((HEAD_DIM, 1)),
        pl.BlockSpec((HALF, tm), lambda i: (0, i % nsb)), pl.BlockSpec((HALF, tm), lambda i: (0, i % nsb)),
        _const_spec(lb_fwd.shape), _const_spec(lb_bwd.shape),
    ]
    out_shape = [
        jax.ShapeDtypeStruct((B, ATTN_HEADS, HEAD_DIM, S), BF16),
        jax.ShapeDtypeStruct((B, ATTN_KV_HEADS, S, HEAD_DIM), BF16),
        jax.ShapeDtypeStruct((B, ATTN_KV_HEADS, S), F32),
        jax.ShapeDtypeStruct((B, ATTN_KV_HEADS, nsb, HEAD_DIM, tm), BF16),
        jax.ShapeDtypeStruct((T, HGRN_W), BF16),
        jax.ShapeDtypeStruct((T, HGRN_W), BF16),
        jax.ShapeDtypeStruct((T, HGRN_W), BF16),
        jax.ShapeDtypeStruct((T, HGRN_W), F32),
        jax.ShapeDtypeStruct((T, HGRN_W), BF16),
        jax.ShapeDtypeStruct((T, HGRN_W), F32),
        jax.ShapeDtypeStruct((T, HGRN_W), BF16),
        jax.ShapeDtypeStruct((T, D_MODEL), BF16),
        jax.ShapeDtypeStruct((T, D_MODEL), BF16),
    ]
    out_specs = [
        pl.BlockSpec((1, ATTN_HEADS, HEAD_DIM, tm), lambda i: (i // nsb, 0, 0, i % nsb)),
        pl.BlockSpec((1, ATTN_KV_HEADS, tm, HEAD_DIM), lambda i: (i // nsb, 0, i % nsb, 0)),
        pl.BlockSpec((1, ATTN_KV_HEADS, tm), lambda i: (i // nsb, 0, i % nsb)),
        pl.BlockSpec((1, ATTN_KV_HEADS, 1, HEAD_DIM, tm), lambda i: (i // nsb, 0, i % nsb, 0, 0)),
        tok(HGRN_W), tok(HGRN_W), tok(HGRN_W), tok(HGRN_W), tok(HGRN_W), tok(HGRN_W), tok(HGRN_W),
        tok(D_MODEL), tok(D_MODEL),
    ]
    return pl.pallas_call(
        _inproj_kernel, grid=(T // tm,), in_specs=in_specs, out_specs=out_specs, out_shape=out_shape,
        compiler_params=_params(("parallel",)), name="in_proj",
    )(x2d, g_mix, w_all, gq, gk, cos_t, sin_t, lb_fwd, lb_bwd)


def _attn_kernel(qt_ref, k_ref, kn2_ref, vt_ref, o_ref, shift_ref, l_ref, acc_ref):
    g = pl.program_id(1)
    nk, tk = vt_ref.shape[2], vt_ref.shape[4]

    def keys(j):
        return k_ref[0, 0, pl.ds(pl.multiple_of(j * tk, tk), tk), :]

    kmax2 = jnp.max(kn2_ref[0, pl.ds(g, 1), :], axis=1, keepdims=True)
    for hh in range(GQA_GROUP):
        q = qt_ref[0, hh].astype(F32)
        shift_ref[hh] = jnp.sqrt(jnp.sum(q * q, axis=0, keepdims=True) * kmax2)

    @pl.when(jnp.max(shift_ref[...]) > SAFE_SCORE_BOUND)
    def _():
        shift_ref[...] = jnp.full(shift_ref.shape, NEG_BIG, F32)

        def max_body(j, carry):
            k = keys(j)
            for hh in range(GQA_GROUP):
                st = _dot(k, qt_ref[0, hh])
                shift_ref[hh] = jnp.maximum(shift_ref[hh], jnp.max(st, axis=0, keepdims=True))
            return carry

        lax.fori_loop(0, nk, max_body, 0)

    acc_ref[...] = jnp.zeros(acc_ref.shape, F32)
    l_ref[...] = jnp.zeros(l_ref.shape, F32)
    tq = qt_ref.shape[3]

    def body(j, carry):
        k = keys(j)
        vt = vt_ref[0, 0, j]
        sts = [_dot(k, qt_ref[0, hh]) for hh in range(GQA_GROUP)]
        for hh in range(GQA_GROUP):
            pt = jnp.exp2(sts[hh] - shift_ref[hh])
            l_ref[hh] += jnp.sum(pt.reshape(tk // SUBLANES, SUBLANES, tq), axis=0)
            acc_ref[hh] += _dot(vt, pt.astype(BF16))
        return carry

    lax.fori_loop(0, nk, body, 0, unroll=8)
    outs = [acc_ref[hh] * (1.0 / jnp.sum(l_ref[hh], axis=0, keepdims=True)) for hh in range(GQA_GROUP)]
    o_ref[0] = jnp.concatenate(outs, axis=0).T.astype(BF16)


def _attention(qt, k, kn2, vt, *, B, S, tq):
    nk, tk = vt.shape[2], vt.shape[4]
    gw = GQA_GROUP * HEAD_DIM
    return pl.pallas_call(
        _attn_kernel, grid=(B, ATTN_KV_HEADS, S // tq),
        in_specs=[
            pl.BlockSpec((1, GQA_GROUP, HEAD_DIM, tq), lambda b, g, i: (b, g, 0, i)),
            pl.BlockSpec((1, 1, S, HEAD_DIM), lambda b, g, i: (b, g, 0, 0)),
            pl.BlockSpec((1, ATTN_KV_HEADS, S), lambda b, g, i: (b, 0, 0)),
            pl.BlockSpec((1, 1, nk, HEAD_DIM, tk), lambda b, g, i: (b, g, 0, 0, 0)),
        ],
        out_specs=pl.BlockSpec((1, tq, gw), lambda b, g, i: (b, i, g)),
        out_shape=jax.ShapeDtypeStruct((B, S, ATTN_Q_W), BF16),
        scratch_shapes=[pltpu.VMEM((GQA_GROUP, 1, tq), F32), pltpu.VMEM((GQA_GROUP, SUBLANES, tq), F32),
                        pltpu.VMEM((GQA_GROUP, HEAD_DIM, tq), F32)],
        compiler_params=_params(("parallel", "parallel", "parallel")), name="attn",
    )(qt, k, kn2, vt)


def _hgrn_run_matrix(cb, reverse):
    C = HGRN_CHUNK
    t = np.arange(cb)[:, None]
    s = np.arange(cb)[None, :]
    same = (t // C) == (s // C)
    return jnp.asarray(same & ((s >= t) if reverse else (s <= t)), dtype=BF16)


_TN = (((0,), (0,)), ((), ()))
_NT = (((1,), (1,)), ((), ()))


def _hgrn_kernel(qf_ref, vf_ref, kf_ref, lff_ref, qb_ref, vb_ref, kb_ref, lfb_ref, runf_ref, runb_ref,
                 of_ref, ob_ref, stf_ref, stb_ref):
    @pl.when(pl.program_id(1) == 0)
    def _():
        stf_ref[...] = jnp.zeros(stf_ref.shape, F32)
        stb_ref[...] = jnp.zeros(stb_ref.shape, F32)

    cb = qf_ref.shape[0]
    C = HGRN_CHUNK
    n = cb // C
    row = lax.broadcasted_iota(jnp.int32, (cb, cb), 0)
    col = lax.broadcasted_iota(jnp.int32, (cb, cb), 1)
    same = (row // C) == (col // C)
    dirs = [
        dict(q=qf_ref, v=vf_ref, k=kf_ref, lf=lff_ref, run=runf_ref, o=of_ref, st=stf_ref, rev=False,
             mask=same & (col <= row), last=C - 1, mid=C // 2),
        dict(q=qb_ref, v=vb_ref, k=kb_ref, lf=lfb_ref, run=runb_ref, o=ob_ref, st=stb_ref, rev=True,
             mask=same & (col >= row), last=0, mid=C - 1 - C // 2),
    ]
    heads = [slice(HGRN_DK * hh, HGRN_DK * (hh + 1)) for hh in range(HGRN_HEADS)]

    for d in dirs:
        lf = d["lf"][...]
        lf_hi = lf.astype(BF16)
        lf_lo = (lf - lf_hi.astype(F32)).astype(BF16)
        d["b"] = _dot(d["run"][...], lf_hi) + _dot(d["run"][...], lf_lo)
    for d in dirs:
        b = d["b"]
        b3 = b.reshape(n, C, HGRN_W)
        b_last = jnp.broadcast_to(b3[:, d["last"]:d["last"] + 1, :], b3.shape).reshape(cb, HGRN_W)
        b_ref = jnp.broadcast_to(b3[:, d["mid"]:d["mid"] + 1, :], b3.shape).reshape(cb, HGRN_W)
        q = d["q"][...].astype(F32)
        k = d["k"][...].astype(F32)
        d["qs"] = (q * jnp.exp(b - b_ref)).astype(BF16)
        d["ks"] = (k * jnp.exp(b_ref - b)).astype(BF16)
        d["qi"] = (q * jnp.exp(b)).astype(BF16)
        d["kst"] = (k * jnp.exp(b_last - b)).astype(BF16)
        d["dec"] = jnp.exp(b_last)
        d["vv"] = d["v"][...]
    for d in dirs:
        d["a"] = [jnp.where(d["mask"], lax.dot_general(d["qs"][:, sl], d["ks"][:, sl], _NT,
                                                       preferred_element_type=F32), 0.0).astype(BF16)
                  for sl in heads]
    for d in dirs:
        d["oi"] = [_dot(d["a"][hh], d["vv"][:, sl]) for hh, sl in enumerate(heads)]
    for d in dirs:
        d["upd"] = [[lax.dot_general(d["vv"][C * ci:C * (ci + 1), sl], d["kst"][C * ci:C * (ci + 1), sl], _TN,
                                     preferred_element_type=F32) for ci in range(n)] for sl in heads]
    for d in dirs:
        order = range(n - 1, -1, -1) if d["rev"] else range(n)
        d["states"] = []
        for hh, sl in enumerate(heads):
            st = d["st"][hh]
            seen = [None] * n
            for ci in order:
                seen[ci] = st.astype(BF16)
                st = d["dec"][C * ci:C * ci + 1, sl] * st + d["upd"][hh][ci]
            d["st"][hh] = st
            d["states"].append(seen)
    for d in dirs:
        for hh, sl in enumerate(heads):
            inter = [lax.dot_general(d["qi"][C * ci:C * (ci + 1), sl], d["states"][hh][ci], _NT,
                                     preferred_element_type=F32) for ci in range(n)]
            d["o"][:, sl] = (d["oi"][hh] + jnp.concatenate(inter, axis=0)).astype(BF16)


def _hgrn(hq, hv, kf, lff, kb, lfb, *, B, S, cb):
    T = B * S
    nblk = S // cb
    fwd = pl.BlockSpec((cb, HGRN_W), lambda b, i: (b * nblk + i, 0))
    bwd = pl.BlockSpec((cb, HGRN_W), lambda b, i: (b * nblk + nblk - 1 - i, 0))
    run = _const_spec((cb, cb))
    st = pltpu.VMEM((HGRN_HEADS, HGRN_DK, HGRN_DK), F32)
    return pl.pallas_call(
        _hgrn_kernel, grid=(B, nblk),
        in_specs=[fwd, fwd, fwd, fwd, bwd, bwd, bwd, bwd, run, run],
        out_specs=[fwd, bwd],
        out_shape=[jax.ShapeDtypeStruct((T, HGRN_W), BF16)] * 2,
        scratch_shapes=[st, st],
        compiler_params=_params(("parallel", "arbitrary")), name="hgrn",
    )(hq, hv, kf, lff, hq, hv, kb, lfb, _hgrn_run_matrix(cb, False), _hgrn_run_matrix(cb, True))


MERGE_SUBTILES = 2

def _merge_kernel(ao_ref, of_ref, ob_ref, og_ref, sa_ref, sb_ref, x_ref, wa_ref, wb_ref, wo_ref,
                  gh_ref, gf_ref, wr_ref, x1_ref, h2_ref, lg_ref):
    tm = x_ref.shape[0]
    sub = tm // MERGE_SUBTILES
    rows = [slice(sub * s, sub * (s + 1)) for s in range(MERGE_SUBTILES)]
    ya = [_dot(ao_ref[r, :], wa_ref[...]) for r in rows]
    hn = []
    for r in rows:
        o = of_ref[r, :].astype(F32) + ob_ref[r, :].astype(F32)
        parts = []
        for hh in range(HGRN_HEADS):
            oh = o[:, HGRN_DK * hh:HGRN_DK * (hh + 1)]
            parts.append(oh * lax.rsqrt(jnp.mean(oh * oh, axis=1, keepdims=True) + EPS))
        hn.append((jnp.concatenate(parts, axis=1) * gh_ref[...] * og_ref[r, :].astype(F32)).astype(BF16))
    yb = [_dot(h, wb_ref[...]) for h in hn]
    merged = [(sa_ref[r, :].astype(F32) * a + sb_ref[r, :].astype(F32) * b).astype(BF16)
              for r, a, b in zip(rows, ya, yb)]
    x1 = [x_ref[r, :] + _dot(m, wo_ref[...]) for r, m in zip(rows, merged)]
    h2 = [v * lax.rsqrt(jnp.mean(v * v, axis=1, keepdims=True) + EPS) * gf_ref[...] for v in x1]
    hi = [v.astype(BF16) for v in h2]
    lo = [(v - h.astype(F32)).astype(BF16) for v, h in zip(h2, hi)]
    lg = [_dot(h, wr_ref[...]) + _dot(l, wr_ref[...]) for h, l in zip(hi, lo)]
    for s, r in enumerate(rows):
        x1_ref[r, :] = x1[s]
        lg_ref[r, :] = lg[s]
        _store_token_tiles(h2_ref.at[pl.ds(sub * ROW_TILES * s, sub * ROW_TILES), :], h2[s])


def _merge(ao, of, ob, og, sa, sb, x2d, wa, wb, wo, gh, gf, wr, *, tm):
    T = x2d.shape[0]
    tok = lambda w: pl.BlockSpec((tm, w), lambda i: (i, 0))
    return pl.pallas_call(
        _merge_kernel, grid=(T // tm,),
        in_specs=[tok(ATTN_Q_W), tok(HGRN_W), tok(HGRN_W), tok(HGRN_W), tok(D_MODEL), tok(D_MODEL), tok(D_MODEL),
                  _const_spec(wa.shape), _const_spec(wb.shape), _const_spec(wo.shape),
                  _const_spec(gh.shape), _const_spec(gf.shape), _const_spec(wr.shape)],
        out_specs=[tok(D_MODEL), pl.BlockSpec((tm * ROW_TILES, LANES), lambda i: (i, 0)), tok(ROUTE_W)],
        out_shape=[jax.ShapeDtypeStruct((T, D_MODEL), F32), jax.ShapeDtypeStruct((T * ROW_TILES, LANES), ROW_DT),
                   jax.ShapeDtypeStruct((T, ROUTE_W), F32)],
        compiler_params=_params(("parallel",)), name="merge",
    )(ao, of, ob, og, sa, sb, x2d, wa, wb, wo, gh, gf, wr)


def _first_argmax(vals, n):
    top = jnp.max(vals, axis=0, keepdims=True)
    rows = lax.broadcasted_iota(jnp.int32, vals.shape, 0)
    idx = jnp.min(jnp.where(vals == top, rows, n), axis=0, keepdims=True)
    return top, idx, rows


def _route_kernel(lg_ref, bias_ref, ids_ref, c1_ref, c2_ref):
    lt = lg_ref[...].T + bias_ref[...]
    gl = lt[0:N_GROUPS]
    ge = jnp.exp(gl - jnp.max(gl, axis=0, keepdims=True))
    gp = ge / jnp.sum(ge, axis=0, keepdims=True)
    p_g, g_idx, _ = _first_argmax(gp, N_GROUPS)
    esel = lt[EXPERT_ROW0:EXPERT_ROW0 + EPG]
    for gg in range(1, N_GROUPS):
        esel = jnp.where(g_idx == gg, lt[EXPERT_ROW0 + EPG * gg:EXPERT_ROW0 + EPG * (gg + 1)], esel)
    ee = jnp.exp(esel - jnp.max(esel, axis=0, keepdims=True))
    pe = ee / jnp.sum(ee, axis=0, keepdims=True)
    p1, i1, rows = _first_argmax(pe, EPG)
    p2, i2, _ = _first_argmax(jnp.where(rows == i1, -1.0, pe), EPG)
    den = p1 + p2
    c1 = p_g * (p1 / den)
    c2 = p_g * (p2 / den)
    e1 = g_idx * EPG + i1
    e2 = g_idx * EPG + i2
    tm = lt.shape[1]
    ids_ref[...] = jnp.concatenate([e1, e2, jnp.zeros((SUBLANES - TOP_K, tm), jnp.int32)], axis=0)
    c1_ref[...] = jnp.broadcast_to(c1, (LANES, tm)).T
    c2_ref[...] = jnp.broadcast_to(c2, (LANES, tm)).T


def _route(lg, bias, *, tm):
    T = lg.shape[0]
    tok = pl.BlockSpec((tm, LANES), lambda i: (i, 0))
    return pl.pallas_call(
        _route_kernel, grid=(T // tm,),
        in_specs=[pl.BlockSpec((tm, ROUTE_W), lambda i: (i, 0)), _const_spec((ROUTE_W, 1))],
        out_specs=[pl.BlockSpec((SUBLANES, tm), lambda i: (0, i)), tok, tok],
        out_shape=[jax.ShapeDtypeStruct((SUBLANES, T), jnp.int32), jax.ShapeDtypeStruct((T, LANES), F32),
                   jax.ShapeDtypeStruct((T, LANES), F32)],
        compiler_params=_params(("parallel",)), name="route",
    )(lg, bias)


def _routing_tables(ids, tmx):
    T = ids.shape[1]
    P = TOP_K * T
    ef = ids.reshape(P)
    onehot = (ef[:, None] == jnp.arange(N_EXPERTS, dtype=jnp.int32)[None, :]).astype(jnp.int32)
    csum = jnp.cumsum(onehot, axis=0)
    cnt = csum[-1]
    rank = jnp.take_along_axis(csum, ef[:, None], axis=1)[:, 0] - 1
    padded = ((cnt + tmx - 1) // tmx) * tmx
    off = jnp.cumsum(padded) - padded
    pos = (off[ef] + rank).astype(jnp.int32)
    n_tiles = (P + N_EXPERTS * tmx) // tmx
    ends = off + padded
    tile_start = jnp.arange(n_tiles, dtype=jnp.int32) * tmx
    tile_e = jnp.sum((tile_start[:, None] >= ends[None, :]).astype(jnp.int32), axis=1)
    tile_valid = (tile_e < N_EXPERTS).astype(jnp.int32)
    tile_e = jnp.minimum(tile_e, N_EXPERTS - 1)
    used = jnp.sum(padded)
    pad_start = jnp.concatenate([off + cnt, used[None]]).astype(jnp.int32)
    n_pad = jnp.concatenate([padded - cnt, (n_tiles * tmx - used)[None]]).astype(jnp.int32)
    return pos.reshape(TOP_K, T), tile_e, tile_valid, pad_start, n_pad, n_tiles


def _dispatch_kernel(pad_start_ref, n_pad_ref, p1_ref, p2_ref, h_ref, xs_hbm, zero_ref, sem, zsem):
    i = pl.program_id(0)
    td = p1_ref.shape[2]

    @pl.when(i == 0)
    def _():
        zero_ref[...] = jnp.zeros(zero_ref.shape, ROW_DT)

        def per_range(e, carry):
            first = pad_start_ref[e]

            def fill(r, c):
                pltpu.make_async_copy(zero_ref, _row_tile(xs_hbm, first + r), zsem).start()
                return c

            def drain(r, c):
                pltpu.make_async_copy(zero_ref, _row_tile(xs_hbm, first), zsem).wait()
                return c

            lax.fori_loop(0, n_pad_ref[e], fill, 0)
            lax.fori_loop(0, n_pad_ref[e], drain, 0)
            return carry

        lax.fori_loop(0, pad_start_ref.shape[0], per_range, 0)

    def issue(r, carry):
        src = _row_tile(h_ref, r)
        pltpu.make_async_copy(src, _row_tile(xs_hbm, p1_ref[0, 0, r]), sem).start(priority=0)
        pltpu.make_async_copy(src, _row_tile(xs_hbm, p2_ref[0, 0, r]), sem).start(priority=1)
        return carry

    lax.fori_loop(0, td, issue, 0, unroll=8)
    for _ in range(TOP_K):
        pltpu.make_async_copy(h_ref, xs_hbm.at[pl.ds(0, td * ROW_TILES), :], sem).wait()


def _dispatch(pos, h2, pad_start, n_pad, n_rows, *, td):
    T = pos.shape[1]
    n = T // td
    idx = pl.BlockSpec((1, 1, td), lambda i, ps, npd: (i, 0, 0), memory_space=pltpu.SMEM)
    grid_spec = pltpu.PrefetchScalarGridSpec(
        num_scalar_prefetch=2, grid=(n,),
        in_specs=[idx, idx, pl.BlockSpec((td * ROW_TILES, LANES), lambda i, ps, npd: (i, 0))],
        out_specs=pl.BlockSpec(memory_space=pl.ANY),
        scratch_shapes=[pltpu.VMEM((ROW_TILES, LANES), ROW_DT), pltpu.SemaphoreType.DMA(()),
                        pltpu.SemaphoreType.DMA(())])
    return pl.pallas_call(
        _dispatch_kernel, grid_spec=grid_spec,
        out_shape=jax.ShapeDtypeStruct((n_rows * ROW_TILES, LANES), ROW_DT),
        compiler_params=_params(("arbitrary",)), name="dispatch",
    )(pad_start, n_pad, pos[0].reshape(n, 1, td), pos[1].reshape(n, 1, td), h2)


def _start_row_gather(idx_ref, src_hbm, dst, sem, n_rows):
    def issue(p, carry):
        for prio in range(2):
            r = 2 * p + prio
            pltpu.make_async_copy(_row_tile(src_hbm, idx_ref[0, 0, r]), _row_tile(dst, r), sem).start(priority=prio)
        return carry

    lax.fori_loop(0, n_rows // 2, issue, 0, unroll=4)


def _wait_row_gather(src_hbm, dst, sem, n_rows):
    pltpu.make_async_copy(src_hbm.at[pl.ds(0, n_rows * ROW_TILES), :], dst, sem).wait()


def _moe_kernel(te_ref, tv_ref, x_ref, wg_ref, wu_ref, wd_ref, y_ref, wg_s, wu_s, wd_s):
    i = pl.program_id(0)

    @pl.when((i == 0) | (te_ref[i] != te_ref[jnp.maximum(i - 1, 0)]))
    def _():
        wg_s[...] = wg_ref[0].astype(BF16)
        wu_s[...] = wu_ref[0].astype(BF16)
        wd_s[...] = wd_ref[0].astype(BF16)

    @pl.when(tv_ref[i] == 1)
    def _():
        x = _load_token_tiles(x_ref).astype(BF16)
        hid = _silu(_dot(x, wg_s[...])) * _dot(x, wu_s[...])
        _store_token_tiles(y_ref, _dot(hid.astype(BF16), wd_s[...]))

    @pl.when(tv_ref[i] == 0)
    def _():
        y_ref[...] = jnp.zeros(y_ref.shape, ROW_DT)


def _moe(xs, tile_e, tile_valid, wg, wu, wd, *, tmx):
    n_tiles = tile_e.shape[0]
    rows = pl.BlockSpec((tmx * ROW_TILES, LANES), lambda i, te, tv: (i, 0))
    rows_in = pl.BlockSpec((tmx * ROW_TILES, LANES), lambda i, te, tv: (i * tv[i], 0))
    grid_spec = pltpu.PrefetchScalarGridSpec(
        num_scalar_prefetch=2, grid=(n_tiles,),
        in_specs=[rows_in,
                  pl.BlockSpec((1, D_MODEL, D_EXPERT), lambda i, te, tv: (te[i], 0, 0)),
                  pl.BlockSpec((1, D_MODEL, D_EXPERT), lambda i, te, tv: (te[i], 0, 0)),
                  pl.BlockSpec((1, D_EXPERT, D_MODEL), lambda i, te, tv: (te[i], 0, 0))],
        out_specs=rows,
        scratch_shapes=[pltpu.VMEM((D_MODEL, D_EXPERT), BF16), pltpu.VMEM((D_MODEL, D_EXPERT), BF16),
                        pltpu.VMEM((D_EXPERT, D_MODEL), BF16)])
    return pl.pallas_call(
        _moe_kernel, grid_spec=grid_spec,
        out_shape=jax.ShapeDtypeStruct((n_tiles * tmx * ROW_TILES, LANES), ROW_DT),
        compiler_params=_params(("arbitrary",)), name="moe",
    )(tile_e, tile_valid, xs, wg, wu, wd)


def _combine_kernel(p1c_ref, p2c_ref, p1n_ref, p2n_ref, x1_ref, c1_ref, c2_ref, y_hbm, o_ref, ybuf, sem):
    i = pl.program_id(0)
    n = pl.num_programs(0)
    tm = o_ref.shape[0]
    slot = i % 2

    def start(p1_ref, p2_ref, s):
        _start_row_gather(p1_ref, y_hbm, ybuf.at[s, 0], sem.at[s], tm)
        _start_row_gather(p2_ref, y_hbm, ybuf.at[s, 1], sem.at[s], tm)

    @pl.when(i == 0)
    def _():
        start(p1c_ref, p2c_ref, 0)

    @pl.when(i + 1 < n)
    def _():
        start(p1n_ref, p2n_ref, 1 - slot)

    _wait_row_gather(y_hbm, ybuf.at[slot, 0], sem.at[slot], tm)
    _wait_row_gather(y_hbm, ybuf.at[slot, 1], sem.at[slot], tm)
    reps = D_MODEL // LANES
    c1 = jnp.tile(c1_ref[...], (1, reps))
    c2 = jnp.tile(c2_ref[...], (1, reps))
    o_ref[...] = (x1_ref[...] + c1 * _load_token_tiles(ybuf.at[slot, 0])
                  + c2 * _load_token_tiles(ybuf.at[slot, 1]))


def _combine(pos, x1, c1, c2, y, *, tm):
    T = x1.shape[0]
    n = T // tm
    p1 = pos[0].reshape(n, 1, tm)
    p2 = pos[1].reshape(n, 1, tm)
    cur = pl.BlockSpec((1, 1, tm), lambda i: (i, 0, 0), memory_space=pltpu.SMEM)
    nxt = pl.BlockSpec((1, 1, tm), lambda i: (jnp.minimum(i + 1, n - 1), 0, 0), memory_space=pltpu.SMEM)
    tok = lambda w: pl.BlockSpec((tm, w), lambda i: (i, 0))
    return pl.pallas_call(
        _combine_kernel, grid=(n,),
        in_specs=[cur, cur, nxt, nxt, tok(D_MODEL), tok(LANES), tok(LANES), pl.BlockSpec(memory_space=pl.ANY)],
        out_specs=tok(D_MODEL),
        out_shape=jax.ShapeDtypeStruct((T, D_MODEL), F32),
        scratch_shapes=[pltpu.VMEM((2, TOP_K, tm * ROW_TILES, LANES), ROW_DT), pltpu.SemaphoreType.DMA((2,))],
        compiler_params=_params(("arbitrary",)), name="combine",
    )(p1, p2, p1, p2, x1, c1, c2, y)


def _tiles(S):
    return dict(tm=min(512, S), tq=min(512, S), cb=min(256, S), tmerge=min(512, S), troute=min(1024, S),
                tmoe=512, tcomb=512, tdisp=min(2048, S))


def _rope_tables(S):
    rows = S // GRID_W
    row_ids = jnp.repeat(jnp.arange(rows), GRID_W).astype(F32)
    col_ids = jnp.tile(jnp.arange(GRID_W), rows).astype(F32)
    inv_freq = ROPE_THETA ** (-jnp.arange(0, HALF, 2, dtype=F32) / HALF)
    ang = jnp.concatenate([row_ids[:, None] * inv_freq, col_ids[:, None] * inv_freq], axis=-1)
    return jnp.cos(ang), jnp.sin(ang)


def kernel(x, g_mix, w_in, q_norm, k_norm, hgrn_norm, lb_fwd, lb_bwd, w_attn_branch, w_hgrn_branch, w_out, g_ffn,
           w_router_group, b_router_group, w_router_expert, b_router_expert, w_exp_gate, w_exp_up, w_exp_down):
    B, S, D = x.shape
    assert D == D_MODEL and w_in.shape == (1, D_MODEL, D_IN) and lb_fwd.shape[0] == 2
    T = B * S
    t = _tiles(S)

    perm = np.concatenate([np.arange(0, HEAD_DIM, 2), np.arange(1, HEAD_DIM, 2)])
    qcols = (np.arange(ATTN_HEADS)[:, None] * HEAD_DIM + perm[None, :]).reshape(-1)
    kcols = IN_OFF[1] + (np.arange(ATTN_KV_HEADS)[:, None] * HEAD_DIM + perm[None, :]).reshape(-1)
    cols = np.concatenate([qcols, kcols, np.arange(IN_OFF[2], D_IN)])
    w_all = w_in[0][:, cols].astype(BF16)
    gq = q_norm[0][perm][:, None].astype(F32)
    gk = k_norm[0][perm][:, None].astype(F32)
    cos, sin = _rope_tables(S)

    (qt, k, kn2, vt, hq, hv, kf, lff, kb, lfb, og, sa, sb) = _in_proj(
        x.reshape(T, D), g_mix.astype(F32), w_all, gq, gk, cos.T, sin.T,
        lb_fwd.astype(F32), lb_bwd.astype(F32), B=B, S=S, tm=t["tm"])

    ao = _attention(qt, k, kn2, vt, B=B, S=S, tq=t["tq"]).reshape(T, ATTN_Q_W)
    of, ob = _hgrn(hq, hv, kf, lff, kb, lfb, B=B, S=S, cb=t["cb"])

    wr = jnp.zeros((D_MODEL, ROUTE_W), F32)
    wr = wr.at[:, 0:N_GROUPS].set(w_router_group[0]).at[:, EXPERT_ROW0:EXPERT_ROW0 + N_EXPERTS].set(w_router_expert[0])
    rbias = jnp.zeros((ROUTE_W, 1), F32)
    rbias = rbias.at[0:N_GROUPS, 0].set(b_router_group[0]).at[EXPERT_ROW0:EXPERT_ROW0 + N_EXPERTS, 0].set(
        b_router_expert[0])
    x1, h2, lg = _merge(
        ao, of, ob, og, sa, sb, x.reshape(T, D), w_attn_branch[0].astype(BF16), w_hgrn_branch[0].astype(BF16),
        w_out[0].astype(BF16), jnp.tile(hgrn_norm[0], HGRN_HEADS)[None, :].astype(F32), g_ffn.astype(F32),
        wr.astype(BF16), tm=t["tmerge"])
    ids, c1, c2 = _route(lg, rbias, tm=t["troute"])
    pos, tile_e, tile_valid, pad_start, n_pad, n_tiles = _routing_tables(ids[0:TOP_K], t["tmoe"])
    xs = _dispatch(pos, h2, pad_start, n_pad, n_tiles * t["tmoe"], td=t["tdisp"])
    y = _moe(xs, tile_e, tile_valid, w_exp_gate[0], w_exp_up[0], w_exp_down[0], tmx=t["tmoe"])
    out = _combine(pos, x1, c1, c2, y, tm=t["tcomb"])
    return out.reshape(B, S, D)
```
